```python
import math
import jax
import jax.numpy as jnp
from jax import lax
import numpy as np

D_MODEL = 1024
BATCH = 2
SEQ = 8192
DEPTH = 1
DEC_BATCH = 32
DEC_SEQ = 8
PAST_LEN = 16384
PAGE_SIZE = 128

N_HEADS_A = D_MODEL // 128
HEAD_DIM_A = 64
V_DIM_A = 2 * HEAD_DIM_A
WIDTH_A = N_HEADS_A * V_DIM_A
N_HEADS_B = D_MODEL // 128
HEAD_DIM_B = 64
WIDTH_B = N_HEADS_B * HEAD_DIM_B
N_IDX_HEADS = D_MODEL // 128
IDX_DIM = 64
TOPK_MAX = 256
D_FF = ((8 * D_MODEL // 3 + 127) // 128) * 128
ROPE_THETA = 10000.0
Q_BLOCK = 128
LN_EPS = 1e-5
DEEPNORM_ALPHA = (2 * DEPTH) ** 0.25
DEEPNORM_BETA = (8 * DEPTH) ** -0.25
QK_A_WIDTH = N_HEADS_A * 2 * HEAD_DIM_A
IN_SIZES = (QK_A_WIDTH, QK_A_WIDTH, WIDTH_A, WIDTH_B, WIDTH_B, WIDTH_B, N_IDX_HEADS * IDX_DIM, IDX_DIM, N_IDX_HEADS, 2 * D_MODEL)
IN_WIDTH = sum(IN_SIZES)

kernel_name = 'hybrid_diffattn_dsa_macaron_step'


def lambda_init(layer):
    return 0.8 - 0.6 * math.exp(-0.3 * layer)


def layer_norm(x, g, b):
    x32 = x.astype(jnp.float32)
    mu = jnp.mean(x32, axis=-1, keepdims=True)
    var = jnp.mean(jnp.square(x32 - mu), axis=-1, keepdims=True)
    return ((x32 - mu) * lax.rsqrt(var + LN_EPS) * g.astype(jnp.float32) + b.astype(jnp.float32)).astype(x.dtype)


def swiglu(x, w_gate_up, w_down):
    g, u = jnp.split(x @ w_gate_up, 2, axis=-1)
    return (jax.nn.silu(g) * u) @ w_down


def rope(x, pos):
    half = x.shape[-1] // 2
    freqs = ROPE_THETA ** (-jnp.arange(half, dtype=jnp.float32) / half)
    ang = pos.astype(jnp.float32)[:, None] * freqs[None, :]
    ang = ang.reshape((pos.shape[0],) + (1,) * (x.ndim - 3) + (half,))
    cos, sin = jnp.cos(ang), jnp.sin(ang)
    x32 = x.astype(jnp.float32)
    x1, x2 = x32[..., :half], x32[..., half:]
    return jnp.concatenate([x1 * cos - x2 * sin, x2 * cos + x1 * sin], axis=-1).astype(x.dtype)


def mixer_inputs(h, w_in, b_gate, pos):
    b, t, _ = h.shape
    splits = [int(c) for c in np.cumsum(IN_SIZES)[:-1]]
    qa, ka, va, qb, kb, vb, iq, ik, iw, gl = jnp.split(h @ w_in, splits, axis=-1)
    qa = rope(qa.reshape(b, t, N_HEADS_A, 2, HEAD_DIM_A), pos)
    ka = rope(ka.reshape(b, t, N_HEADS_A, 2, HEAD_DIM_A), pos)
    va = va.reshape(b, t, N_HEADS_A, V_DIM_A)
    qb = rope(qb.reshape(b, t, N_HEADS_B, HEAD_DIM_B), pos)
    kb = rope(kb.reshape(b, t, N_HEADS_B, HEAD_DIM_B), pos)
    vb = vb.reshape(b, t, N_HEADS_B, HEAD_DIM_B)
    iq = rope(iq.reshape(b, t, N_IDX_HEADS, IDX_DIM), pos)
    ik = rope(ik, pos)
    iw = iw * N_IDX_HEADS ** -0.5
    gates = jax.nn.sigmoid(gl + b_gate).reshape(b, t, 2, D_MODEL)
    return qa, ka, va, qb, kb, vb, iq, ik, iw, gates


def diff_weights(logits, lam):
    p = jax.nn.softmax(logits, axis=-1)
    return p[..., 0, :, :] - lam * p[..., 1, :, :]


def diff_head_norm(o, g, lam_init):
    o32 = o.astype(jnp.float32)
    o32 = o32 * lax.rsqrt(jnp.mean(o32 * o32, axis=-1, keepdims=True) + LN_EPS)
    return (o32 * g.astype(jnp.float32) * (1.0 - lam_init)).astype(o.dtype)


def diff_attn_prompt(qa, ka, va, lam):
    b, s = qa.shape[0], qa.shape[1]
    key_pos = jnp.arange(s)
    scale = HEAD_DIM_A ** -0.5

    def block(i):
        start = i * Q_BLOCK
        q = lax.dynamic_slice_in_dim(qa, start, Q_BLOCK, axis=1)
        logits = jnp.einsum('bthcd,bshcd->bhcts', q, ka).astype(jnp.float32) * scale
        q_pos = start + jnp.arange(Q_BLOCK)
        causal = key_pos[None, :] <= q_pos[:, None]
        a = diff_weights(jnp.where(causal, logits, -jnp.inf), lam)
        return jnp.einsum('bhts,bshe->bthe', a.astype(va.dtype), va)

    out = lax.map(block, jnp.arange(s // Q_BLOCK))
    return out.transpose(1, 0, 2, 3, 4).reshape(b, s, N_HEADS_A, V_DIM_A)


def diff_attn_sample(qa, ka, va, pool_k, pool_v, layer, page_table, lam):
    db, t = qa.shape[0], qa.shape[1]
    n_pages = page_table.shape[1]
    past = n_pages * PAGE_SIZE
    scale = HEAD_DIM_A ** -0.5

    def page_logits(phys):
        return jnp.einsum('bthcd,bshcd->bhcts', qa, pool_k[layer, phys])

    lp = lax.map(page_logits, page_table.T)
    lp = lp.transpose(1, 2, 3, 4, 0, 5).reshape(db, N_HEADS_A, 2, t, past).astype(jnp.float32) * scale
    ln = jnp.einsum('bthcd,bshcd->bhcts', qa, ka).astype(jnp.float32) * scale
    causal = jnp.arange(t)[None, :] <= jnp.arange(t)[:, None]
    ln = jnp.where(causal, ln, -jnp.inf)
    a = diff_weights(jnp.concatenate([lp, ln], axis=-1), lam)
    a_pages = a[..., :past].reshape(db, N_HEADS_A, t, n_pages, PAGE_SIZE).transpose(3, 0, 1, 2, 4).astype(va.dtype)

    def page_values(args):
        phys, a_blk = args
        return jnp.einsum('bhts,bshe->bthe', a_blk, pool_v[layer, phys])

    o_past = lax.map(page_values, (page_table.T, a_pages)).sum(axis=0, dtype=jnp.float32)
    o_new = jnp.einsum('bhts,bshe->bthe', a[..., past:].astype(va.dtype), va).astype(jnp.float32)
    return (o_past + o_new).astype(va.dtype)


def indexer_scores(iq, ik, iw):
    s = jnp.einsum('bthd,bsd->bths', iq, ik).astype(jnp.float32) * IDX_DIM ** -0.5
    return jnp.einsum('bths,bth->bts', jax.nn.relu(s), iw.astype(jnp.float32))


def sparse_attend(q, k_sel, v_sel, valid):
    s = jnp.einsum('bthd,btkhd->bthk', q, k_sel).astype(jnp.float32) * HEAD_DIM_B ** -0.5
    p = jax.nn.softmax(jnp.where(valid[:, :, None, :], s, -jnp.inf), axis=-1)
    return jnp.einsum('bthk,btkhd->bthd', p.astype(v_sel.dtype), v_sel)


def dsa_prompt(qb, kb, vb, iq, ik, iw, topk):
    b, s = qb.shape[0], qb.shape[1]
    key_pos = jnp.arange(s)
    bidx = jnp.arange(b)[:, None, None]

    def block(i):
        start = i * Q_BLOCK
        sl = lambda a: lax.dynamic_slice_in_dim(a, start, Q_BLOCK, axis=1)
        q_pos = start + jnp.arange(Q_BLOCK)
        scores = indexer_scores(sl(iq), ik, sl(iw))
        scores = jnp.where(key_pos[None, None, :] <= q_pos[None, :, None], scores, -jnp.inf)
        _, sel = lax.top_k(scores, topk)
        valid = sel <= q_pos[None, :, None]
        return sparse_attend(sl(qb), kb[bidx, sel], vb[bidx, sel], valid)

    out = lax.map(block, jnp.arange(s // Q_BLOCK))
    return out.transpose(1, 0, 2, 3, 4).reshape(b, s, N_HEADS_B, HEAD_DIM_B)


def dsa_sample(qb, kb, vb, iq, ik, iw, pool_k, pool_v, pool_ik, layer, page_table, topk):
    db, t = qb.shape[0], qb.shape[1]
    past = page_table.shape[1] * PAGE_SIZE
    ik_all = jnp.concatenate([pool_ik[layer, page_table].reshape(db, past, IDX_DIM), ik], axis=1)
    q_pos = past + jnp.arange(t)
    scores = indexer_scores(iq, ik_all, iw)
    scores = jnp.where(jnp.arange(past + t)[None, None, :] <= q_pos[None, :, None], scores, -jnp.inf)
    _, sel = lax.top_k(scores, topk)
    valid = sel <= q_pos[None, :, None]
    in_past = (sel < past)[..., None, None]
    sel_past = jnp.minimum(sel, past - 1)
    bidx = jnp.arange(db)[:, None, None]
    phys = page_table[bidx, sel_past // PAGE_SIZE]
    slot = sel_past % PAGE_SIZE
    sel_new = jnp.clip(sel - past, 0, t - 1)
    k_sel = jnp.where(in_past, pool_k[layer, phys, slot], kb[bidx, sel_new])
    v_sel = jnp.where(in_past, pool_v[layer, phys, slot], vb[bidx, sel_new])
    return sparse_attend(qb, k_sel, v_sel, valid)


def trunk_layer(x, pos, attend, lam_init, ln_g, ln_b, ffn1_w_gate_up, ffn1_w_down, w_in, b_gate,
                diff_lambda_vecs, diff_subln_g, w_branch_a, w_branch_b, w_o, ffn2_w_gate_up, ffn2_w_down):
    b, t, _ = x.shape
    h = layer_norm(DEEPNORM_ALPHA * x + 0.5 * swiglu(x, ffn1_w_gate_up, ffn1_w_down), ln_g[0], ln_b[0])
    qa, ka, va, qb, kb, vb, iq, ik, iw, gates = mixer_inputs(h, w_in, b_gate, pos)
    lv = diff_lambda_vecs.astype(jnp.float32)
    lam = jnp.exp(jnp.sum(lv[0] * lv[1])) - jnp.exp(jnp.sum(lv[2] * lv[3])) + lam_init
    oa, ob = attend(qa, ka, va, qb, kb, vb, iq, ik, iw, lam)
    ya = diff_head_norm(oa, diff_subln_g, lam_init).reshape(b, t, WIDTH_A) @ w_branch_a
    yb = ob.reshape(b, t, WIDTH_B) @ w_branch_b
    mixed = (gates[..., 0, :] * ya + gates[..., 1, :] * yb) @ w_o
    h = layer_norm(DEEPNORM_ALPHA * h + mixed, ln_g[1], ln_b[1])
    y = layer_norm(DEEPNORM_ALPHA * h + 0.5 * swiglu(h, ffn2_w_gate_up, ffn2_w_down), ln_g[2], ln_b[2])
    return y, (ka, va, kb, vb, ik)


def setup_inputs(seed: int = 0) -> dict:
    key = jax.random.key(seed)
    ks = jax.random.split(key, 24)
    f32 = jnp.float32
    n_pages = PAST_LEN // PAGE_SIZE
    n_used = DEC_BATCH * n_pages
    n_pool = n_used + max(1, n_used // 4)

    def nrm(k, shape, scale):
        return scale * jax.random.normal(k, shape, f32)

    page_table = jax.random.permutation(ks[7], n_pool)[:n_used].reshape(DEC_BATCH, n_pages).astype(jnp.int32)
    return {
        'x_prompt': nrm(ks[0], (BATCH, SEQ, D_MODEL), 1.0),
        'x_sample': nrm(ks[1], (DEC_BATCH, DEC_SEQ, D_MODEL), 1.0),
        'cache_a_k': nrm(ks[2], (DEPTH, n_pool, PAGE_SIZE, N_HEADS_A, 2, HEAD_DIM_A), 1.0),
        'cache_a_v': nrm(ks[3], (DEPTH, n_pool, PAGE_SIZE, N_HEADS_A, V_DIM_A), 1.0),
        'cache_b_k': nrm(ks[4], (DEPTH, n_pool, PAGE_SIZE, N_HEADS_B, HEAD_DIM_B), 1.0),
        'cache_b_v': nrm(ks[5], (DEPTH, n_pool, PAGE_SIZE, N_HEADS_B, HEAD_DIM_B), 1.0),
        'cache_idx_k': nrm(ks[6], (DEPTH, n_pool, PAGE_SIZE, IDX_DIM), 1.0),
        'page_table': page_table,
        'ln_g': 1.0 + nrm(ks[8], (DEPTH, 3, D_MODEL), 0.02),
        'ln_b': nrm(ks[9], (DEPTH, 3, D_MODEL), 0.02),
        'ffn1_w_gate_up': nrm(ks[10], (DEPTH, D_MODEL, 2 * D_FF), D_MODEL ** -0.5),
        'ffn1_w_down': nrm(ks[11], (DEPTH, D_FF, D_MODEL), DEEPNORM_BETA * D_FF ** -0.5),
        'w_in': nrm(ks[12], (DEPTH, D_MODEL, IN_WIDTH), D_MODEL ** -0.5),
        'b_gate': nrm(ks[13], (DEPTH, 2 * D_MODEL), 0.1),
        'diff_lambda_vecs': nrm(ks[14], (DEPTH, 4, HEAD_DIM_A), 0.1),
        'diff_subln_g': 1.0 + nrm(ks[15], (DEPTH, V_DIM_A), 0.02),
        'w_branch_a': nrm(ks[16], (DEPTH, WIDTH_A, D_MODEL), DEEPNORM_BETA * WIDTH_A ** -0.5),
        'w_branch_b': nrm(ks[17], (DEPTH, WIDTH_B, D_MODEL), DEEPNORM_BETA * WIDTH_B ** -0.5),
        'w_o': nrm(ks[18], (DEPTH, D_MODEL, D_MODEL), DEEPNORM_BETA * D_MODEL ** -0.5),
        'ffn2_w_gate_up': nrm(ks[19], (DEPTH, D_MODEL, 2 * D_FF), D_MODEL ** -0.5),
        'ffn2_w_down': nrm(ks[20], (DEPTH, D_FF, D_MODEL), DEEPNORM_BETA * D_FF ** -0.5),
    }


def reference(x_prompt, x_sample, cache_a_k, cache_a_v, cache_b_k, cache_b_v, cache_idx_k, page_table,
              ln_g, ln_b, ffn1_w_gate_up, ffn1_w_down, w_in, b_gate, diff_lambda_vecs, diff_subln_g,
              w_branch_a, w_branch_b, w_o, ffn2_w_gate_up, ffn2_w_down):
    past = page_table.shape[1] * PAGE_SIZE
    pos_prompt = jnp.arange(x_prompt.shape[1])
    pos_sample = past + jnp.arange(x_sample.shape[1])
    topk_prompt = min(TOPK_MAX, x_prompt.shape[1] // 4)
    topk_sample = min(TOPK_MAX, (past + x_sample.shape[1]) // 4)
    h_prompt, h_sample = x_prompt, x_sample
    prompt_states = ([], [], [], [], [])
    sample_states = ([], [], [], [], [])
    for layer in range(DEPTH):
        lam_init = lambda_init(layer)
        weights = (ln_g[layer], ln_b[layer], ffn1_w_gate_up[layer], ffn1_w_down[layer], w_in[layer], b_gate[layer],
                   diff_lambda_vecs[layer], diff_subln_g[layer], w_branch_a[layer], w_branch_b[layer], w_o[layer],
                   ffn2_w_gate_up[layer], ffn2_w_down[layer])

        def attend_prompt(qa, ka, va, qb, kb, vb, iq, ik, iw, lam):
            return (diff_attn_prompt(qa, ka, va, lam), dsa_prompt(qb, kb, vb, iq, ik, iw, topk_prompt))

        def attend_sample(qa, ka, va, qb, kb, vb, iq, ik, iw, lam, layer=layer):
            return (diff_attn_sample(qa, ka, va, cache_a_k, cache_a_v, layer, page_table, lam),
                    dsa_sample(qb, kb, vb, iq, ik, iw, cache_b_k, cache_b_v, cache_idx_k, layer, page_table, topk_sample))

        h_prompt, st_p = trunk_layer(h_prompt, pos_prompt, attend_prompt, lam_init, *weights)
        h_sample, st_s = trunk_layer(h_sample, pos_sample, attend_sample, lam_init, *weights)
        for lst, v in zip(prompt_states, st_p):
            lst.append(v)
        for lst, v in zip(sample_states, st_s):
            lst.append(v)
    new_a_k_prompt, new_a_v_prompt, new_b_k_prompt, new_b_v_prompt, new_idx_k_prompt = [jnp.stack(s) for s in prompt_states]
    new_a_k_sample, new_a_v_sample, new_b_k_sample, new_b_v_sample, new_idx_k_sample = [jnp.stack(s) for s in sample_states]
    return (h_prompt, h_sample, new_a_k_prompt, new_a_v_prompt, new_b_k_prompt, new_b_v_prompt, new_idx_k_prompt,
            new_a_k_sample, new_a_v_sample, new_b_k_sample, new_b_v_sample, new_idx_k_sample)
```

```python
import functools
import math

import jax
import jax.numpy as jnp
from jax import lax
from jax.experimental import pallas as pl
from jax.experimental.pallas import tpu as pltpu

F32 = jnp.float32
BF16 = jnp.bfloat16
I32 = jnp.int32

LANES = 128
LN_EPS = 1e-5
ROPE_THETA = 10000.0
TOPK_MAX = 256
MASKED = -3.0e38
MASKED_TEST = -1.0e38
NEG_LOGIT = -1.0e30
INT_MAX = 2**31 - 1
INT_MIN = -2**31
VMEM_LIMIT = 56 * 1024 * 1024

TM_FFN = 512
TM_PROJ = 512
TQ_DIFF = 512
TK_DIFF = 512
TQ_DSA = 256
TK_DSA = 512
F_CHUNK = 1408


def _lambda_init(layer):
    return 0.8 - 0.6 * math.exp(-0.3 * layer)


def _params(*sem):
    return pltpu.CompilerParams(dimension_semantics=sem, vmem_limit_bytes=VMEM_LIMIT)


def _resident(shape):
    nd = len(shape)
    return pl.BlockSpec(shape, lambda *_: (0,) * nd, pipeline_mode=pl.Buffered(1))


def _dot(a, b):
    return jnp.dot(a, b, preferred_element_type=F32)


def _dot_nt(a, b):
    return lax.dot_general(a, b, (((1,), (1,)), ((), ())), preferred_element_type=F32)


def _layer_norm(x, g, b):
    mu = jnp.mean(x, axis=-1, keepdims=True)
    xc = x - mu
    var = jnp.mean(xc * xc, axis=-1, keepdims=True)
    return xc * lax.rsqrt(var + LN_EPS) * g + b


def _ffn_kernel(x_ref, wg_ref, wu_ref, wd_ref, g_ref, b_ref, o_ref, *, alpha, f_chunk):
    x = x_ref[...]
    xb = x.astype(BF16)
    d_ff = wg_ref.shape[1]
    acc = jnp.zeros(x.shape, F32)
    for c in range(0, d_ff, f_chunk):
        g = _dot(xb, wg_ref[:, c:c + f_chunk])
        u = _dot(xb, wu_ref[:, c:c + f_chunk])
        hmid = (g / (1.0 + jnp.exp(-g))) * u
        acc = acc + _dot(hmid.astype(BF16), wd_ref[c:c + f_chunk, :])
    o_ref[...] = _layer_norm(alpha * x + 0.5 * acc, g_ref[...], b_ref[...])


def _ffn_ln(x, wg, wu, wd, g, b, alpha):
    m, d = x.shape
    d_ff = wg.shape[1]
    tm = min(TM_FFN, m)
    f_chunk = F_CHUNK if d_ff % F_CHUNK == 0 else d_ff
    return pl.pallas_call(
        functools.partial(_ffn_kernel, alpha=alpha, f_chunk=f_chunk),
        out_shape=jax.ShapeDtypeStruct((m, d), F32),
        grid=(m // tm,),
        in_specs=[pl.BlockSpec((tm, d), lambda i: (i, 0)),
                  _resident((d, d_ff)), _resident((d, d_ff)), _resident((d_ff, d)),
                  _resident((1, d)), _resident((1, d))],
        out_specs=pl.BlockSpec((tm, d), lambda i: (i, 0)),
        compiler_params=_params("parallel"),
        name="ffn_ln",
    )(x, wg, wu, wd, g, b)


def _inproj_kernel(h_ref, cos_ref, sin_ref, w_ref, bg_ref,
                   qa_ref, ka_ref, va_ref, qb_ref, kb_ref, vb_ref, iq_ref, ikp_ref, ik_ref, iw_ref, gates_ref,
                   *, widths, q_scale, iw_scale):
    xb = h_ref[0].astype(BF16)
    tm = xb.shape[0]
    cos = cos_ref[...]
    sin = sin_ref[...]
    lane = lax.broadcasted_iota(I32, (tm, LANES), 1)
    first_half = (lane % 64) < 32

    def rope(v):
        rot = jnp.where(first_half, pltpu.roll(v, LANES - 32, 1), pltpu.roll(v, 32, 1))
        return v * cos + rot * sin

    def project(off, width, out_ref, fn):
        for c in range(0, width, LANES):
            out_ref[0, :, c:c + LANES] = fn(_dot(xb, w_ref[:, off + c:off + c + LANES]), c)

    w_qa, w_va, w_qb, w_vb, w_iq, d_model = widths
    off = 0
    project(off, w_qa, qa_ref, lambda v, c: rope(v) * q_scale); off += w_qa
    project(off, w_qa, ka_ref, lambda v, c: rope(v)); off += w_qa
    project(off, w_va, va_ref, lambda v, c: v); off += w_va
    project(off, w_qb, qb_ref, lambda v, c: rope(v) * q_scale); off += w_qb
    project(off, w_qb, kb_ref, lambda v, c: rope(v)); off += w_qb
    project(off, w_vb, vb_ref, lambda v, c: v); off += w_vb
    project(off, w_iq, iq_ref, lambda v, c: rope(v) * q_scale); off += w_iq
    ikp = rope(_dot(xb, w_ref[:, off:off + LANES])); off += LANES
    ikp_ref[0] = ikp
    ik_ref[0] = ikp[:, :ik_ref.shape[2]]
    iw_ref[0] = _dot(xb, w_ref[:, off:off + LANES]) * iw_scale; off += LANES
    project(off, 2 * d_model, gates_ref,
            lambda v, c: 1.0 / (1.0 + jnp.exp(-(v + bg_ref[:, c:c + LANES]))))


def _in_proj(h, cos_t, sin_t, w_all, b_gate, widths, idx_dim, q_scale, iw_scale):
    bsz, s, d = h.shape
    w_qa, w_va, w_qb, w_vb, w_iq, d_model = widths
    tm = min(TM_PROJ, s)
    out_w = (w_qa, w_qa, w_va, w_qb, w_qb, w_vb, w_iq, LANES, idx_dim, LANES, 2 * d_model)
    return pl.pallas_call(
        functools.partial(_inproj_kernel, widths=widths, q_scale=q_scale, iw_scale=iw_scale),
        out_shape=[jax.ShapeDtypeStruct((bsz, s, w), F32) for w in out_w],
        grid=(bsz, s // tm),
        in_specs=[pl.BlockSpec((1, tm, d), lambda b, i: (b, i, 0)),
                  pl.BlockSpec((tm, LANES), lambda b, i: (i, 0)),
                  pl.BlockSpec((tm, LANES), lambda b, i: (i, 0)),
                  _resident(w_all.shape), _resident(b_gate.shape)],
        out_specs=[pl.BlockSpec((1, tm, w), lambda b, i: (b, i, 0)) for w in out_w],
        compiler_params=_params("parallel", "parallel"),
        name="in_proj",
    )(h, cos_t, sin_t, w_all, b_gate)


def _lam(lv_ref, lam_init):
    lv = lv_ref[...]
    s01 = jnp.sum(lv[0:1] * lv[1:2], axis=1, keepdims=True)
    s23 = jnp.sum(lv[2:3] * lv[3:4], axis=1, keepdims=True)
    return jnp.exp(s01) - jnp.exp(s23) + lam_init


def _head_norm(o, g, lam_init):
    return o * lax.rsqrt(jnp.mean(o * o, axis=-1, keepdims=True) + LN_EPS) * g * (1.0 - lam_init)


def _order_key_to_f32(u):
    bits = jnp.where(u < 0, u ^ INT_MIN, ~u)
    return lax.bitcast_convert_type(bits, F32)


def _kth_largest(count_ge, rows, topk, n_cols):
    def body(it, carry):
        prefix, cnt_at = carry
        cand = prefix | jnp.left_shift(jnp.int32(1), 31 - it)
        cnt = count_ge(_order_key_to_f32(cand))
        ok = cnt >= topk
        return jnp.where(ok, cand, prefix), jnp.where(ok, cnt, cnt_at)

    prefix, cnt_at = lax.fori_loop(
        0, 32, body, (jnp.zeros((rows, 1), I32), jnp.full((rows, 1), n_cols, F32)))
    return _order_key_to_f32(prefix), cnt_at


def _tie_bound(count_eq_below, need, n_bits, rows):
    def body(it, bound):
        cand = bound | jnp.left_shift(jnp.int32(1), n_bits - 1 - it)
        return jnp.where(count_eq_below(cand) <= need - 1.0, cand, bound)

    return lax.fori_loop(0, n_bits, body, jnp.zeros((rows, 1), I32))


def _selected(sc, col, thr, bound):
    return ((sc > thr) | ((sc == thr) & (col <= bound))) & (sc > MASKED_TEST)


def _softmax_step(s, mask, m_old, l_old):
    if mask is not None:
        s = jnp.where(mask, s, NEG_LOGIT)
    m_new = jnp.maximum(m_old, jnp.max(s, axis=1, keepdims=True))
    p = jnp.exp(s - m_new)
    if mask is not None:
        p = jnp.where(mask, p, 0.0)
    alpha = jnp.exp(m_old - m_new)
    return p, alpha, m_new, alpha * l_old + jnp.sum(p, axis=1, keepdims=True)


def _diffattn_kernel(lv_ref, g_ref, q_ref, k_ref, v_ref, o_ref, m_ref, l_ref, acc_ref, *, tq, tk, lam_init):
    qi = pl.program_id(2)
    ki = pl.program_id(3)
    last = ((qi + 1) * tq - 1) // tk

    @pl.when(ki == 0)
    def _():
        m_ref[...] = jnp.full(m_ref.shape, NEG_LOGIT, F32)
        l_ref[...] = jnp.zeros(l_ref.shape, F32)
        acc_ref[...] = jnp.zeros(acc_ref.shape, F32)

    @pl.when(ki <= last)
    def _():
        q = q_ref[0]
        lane = lax.broadcasted_iota(I32, q.shape, 1)
        kb = k_ref[0].astype(BF16)
        vb = v_ref[0].astype(BF16)
        row = qi * tq + lax.broadcasted_iota(I32, (tq, tk), 0)
        col = ki * tk + lax.broadcasted_iota(I32, (tq, tk), 1)
        causal = col <= row
        for c in range(2):
            in_half = (lane < 64) if c == 0 else (lane >= 64)
            qc = jnp.where(in_half, q, 0.0).astype(BF16)
            p, alpha, m_new, l_new = _softmax_step(_dot_nt(qc, kb), causal, m_ref[c], l_ref[c])
            acc_ref[c] = alpha * acc_ref[c] + _dot(p.astype(BF16), vb)
            m_ref[c] = m_new
            l_ref[c] = l_new

    @pl.when(ki == last)
    def _():
        o = acc_ref[0] / l_ref[0] - _lam(lv_ref, lam_init) * (acc_ref[1] / l_ref[1])
        o_ref[0] = _head_norm(o, g_ref[...], lam_init)


def _diff_attn_prompt(qa, ka, va, lv, subln_g, lam_init):
    bsz, s, width = qa.shape
    n_heads = width // LANES
    tq, tk = min(TQ_DIFF, s), min(TK_DIFF, s)

    def kv_map(b, h, qi, ki):
        return (b, jnp.minimum(ki, ((qi + 1) * tq - 1) // tk), h)

    return pl.pallas_call(
        functools.partial(_diffattn_kernel, tq=tq, tk=tk, lam_init=lam_init),
        out_shape=jax.ShapeDtypeStruct((bsz, s, width), F32),
        grid=(bsz, n_heads, s // tq, s // tk),
        in_specs=[pl.BlockSpec(lv.shape, lambda b, h, qi, ki: (0, 0)),
                  pl.BlockSpec(subln_g.shape, lambda b, h, qi, ki: (0, 0)),
                  pl.BlockSpec((1, tq, LANES), lambda b, h, qi, ki: (b, qi, h)),
                  pl.BlockSpec((1, tk, LANES), kv_map),
                  pl.BlockSpec((1, tk, LANES), kv_map)],
        out_specs=pl.BlockSpec((1, tq, LANES), lambda b, h, qi, ki: (b, qi, h)),
        scratch_shapes=[pltpu.VMEM((2, tq, 1), F32), pltpu.VMEM((2, tq, 1), F32),
                        pltpu.VMEM((2, tq, LANES), F32)],
        compiler_params=_params("parallel", "parallel", "parallel", "arbitrary"),
        name="diff_attn_prompt",
    )(lv, subln_g, qa, ka, va)


def _dsa_kernel(iq_ref, iw_ref, ik_ref, q_ref, k_ref, v_ref, o_ref,
                sc_ref, thr_ref, bound_ref, m_ref, l_ref, acc_ref, *, tq, tk, topk, n_idx_heads, n_col_bits):
    qi = pl.program_id(1)
    ki = pl.program_id(2)
    last = ((qi + 1) * tq - 1) // tk
    n_need = last + 1
    row = qi * tq + lax.broadcasted_iota(I32, (tq, tk), 0)
    col0 = lax.broadcasted_iota(I32, (tq, tk), 1)

    @pl.when(ki == 0)
    def _():
        m_ref[...] = jnp.full(m_ref.shape, NEG_LOGIT, F32)
        l_ref[...] = jnp.zeros(l_ref.shape, F32)
        acc_ref[...] = jnp.zeros(acc_ref.shape, F32)
        iw = iw_ref[0]

        def score_chunk(c, carry):
            kc = ik_ref[0, pl.ds(pl.multiple_of(c * tk, tk), tk), :].astype(BF16)
            acc = jnp.zeros((tq, tk), F32)
            for h in range(n_idx_heads):
                qh = iq_ref[0, :, h * LANES:(h + 1) * LANES].astype(BF16)
                acc = acc + jnp.maximum(_dot_nt(qh, kc), 0.0) * iw[:, h:h + 1]
            sc_ref[c] = jnp.where(c * tk + col0 <= row, acc, MASKED)
            return carry

        lax.fori_loop(0, n_need, score_chunk, 0)

        def count(pred):
            def body(c, cnt):
                return cnt + jnp.sum(jnp.where(pred(sc_ref[c], c), 1.0, 0.0), axis=1, keepdims=True)
            return lax.fori_loop(0, n_need, body, jnp.zeros((tq, 1), F32))

        thr, n_ge = _kth_largest(lambda t: count(lambda sc, c: sc >= t), tq, topk, n_need * tk)
        thr_ref[...] = thr
        bound_ref[...] = jnp.full((tq, 1), INT_MAX, I32)

        @pl.when(jnp.max(n_ge) > topk)
        def _():
            need = topk - count(lambda sc, c: sc > thr)
            bound_ref[...] = _tie_bound(
                lambda j: count(lambda sc, c: (sc == thr) & (c * tk + col0 < j)), need, n_col_bits, tq)

    @pl.when(ki <= last)
    def _():
        sel = _selected(sc_ref[ki], ki * tk + col0, thr_ref[...], bound_ref[...])
        lane = lax.broadcasted_iota(I32, (tq, LANES), 1)
        for pair in range(q_ref.shape[2] // LANES):
            cols = slice(pair * LANES, (pair + 1) * LANES)
            q2 = q_ref[0, :, cols]
            k2 = k_ref[0, :, cols].astype(BF16)
            v2 = v_ref[0, :, cols].astype(BF16)
            for c in range(2):
                h = 2 * pair + c
                in_half = (lane < 64) if c == 0 else (lane >= 64)
                qc = jnp.where(in_half, q2, 0.0).astype(BF16)
                p, alpha, m_new, l_new = _softmax_step(_dot_nt(qc, k2), sel, m_ref[h], l_ref[h])
                acc_ref[h] = alpha * acc_ref[h] + _dot(p.astype(BF16), v2)
                m_ref[h] = m_new
                l_ref[h] = l_new

    @pl.when(ki == last)
    def _():
        lane = lax.broadcasted_iota(I32, (tq, LANES), 1)
        for pair in range(q_ref.shape[2] // LANES):
            lo = acc_ref[2 * pair] / l_ref[2 * pair]
            hi = acc_ref[2 * pair + 1] / l_ref[2 * pair + 1]
            o_ref[0, :, pair * LANES:(pair + 1) * LANES] = jnp.where(lane < 64, lo, hi)


def _dsa_prompt(qb, kb, vb, iq, ikp, iw, topk, n_idx_heads):
    bsz, s, width = qb.shape
    tq, tk = min(TQ_DSA, s), min(TK_DSA, s)
    n_heads = 2 * (width // LANES)

    def kv_map(b, qi, ki):
        return (b, jnp.minimum(ki, ((qi + 1) * tq - 1) // tk), 0)

    return pl.pallas_call(
        functools.partial(_dsa_kernel, tq=tq, tk=tk, topk=topk, n_idx_heads=n_idx_heads,
                          n_col_bits=max(1, (s - 1).bit_length())),
        out_shape=jax.ShapeDtypeStruct((bsz, s, width), F32),
        grid=(bsz, s // tq, s // tk),
        in_specs=[pl.BlockSpec((1, tq, iq.shape[2]), lambda b, qi, ki: (b, qi, 0)),
                  pl.BlockSpec((1, tq, LANES), lambda b, qi, ki: (b, qi, 0)),
                  pl.BlockSpec((1, s, LANES), lambda b, qi, ki: (b, 0, 0)),
                  pl.BlockSpec((1, tq, width), lambda b, qi, ki: (b, qi, 0)),
                  pl.BlockSpec((1, tk, width), kv_map),
                  pl.BlockSpec((1, tk, width), kv_map)],
        out_specs=pl.BlockSpec((1, tq, width), lambda b, qi, ki: (b, qi, 0)),
        scratch_shapes=[pltpu.VMEM((s // tk, tq, tk), F32),
                        pltpu.VMEM((tq, 1), F32), pltpu.VMEM((tq, 1), I32),
                        pltpu.VMEM((n_heads, tq, 1), F32), pltpu.VMEM((n_heads, tq, 1), F32),
                        pltpu.VMEM((n_heads, tq, LANES), F32)],
        compiler_params=_params("parallel", "parallel", "arbitrary"),
        name="dsa_prompt",
    )(iq, iw, ikp, qb, kb, vb)


def _idx_scores(iq, kc, iw, n_idx_heads):
    w = kc.shape[1]
    acc = jnp.zeros((iq.shape[0], kc.shape[0]), F32)
    for h in range(n_idx_heads):
        qh = iq[:, h * LANES:h * LANES + w].astype(BF16)
        acc = acc + jnp.maximum(_dot_nt(qh, kc), 0.0) * iw[:, h:h + 1]
    return acc


def _sample_scores_kernel(pt_ref, iq_ref, iw_ref, ik_ref, o_ref, *, n_idx_heads):
    o_ref[0] = _idx_scores(iq_ref[0], ik_ref[0].astype(BF16), iw_ref[0], n_idx_heads)


def _sample_scores(page_table, iq, iw, pool_ik, n_idx_heads):
    bsz, t, _ = iq.shape
    n_pages = page_table.shape[1]
    page, idx_dim = pool_ik.shape[1], pool_ik.shape[2]
    return pl.pallas_call(
        functools.partial(_sample_scores_kernel, n_idx_heads=n_idx_heads),
        out_shape=jax.ShapeDtypeStruct((bsz, t, n_pages * page), F32),
        grid_spec=pltpu.PrefetchScalarGridSpec(
            num_scalar_prefetch=1,
            grid=(bsz, n_pages),
            in_specs=[pl.BlockSpec((1, t, iq.shape[2]), lambda b, p, pt: (b, 0, 0)),
                      pl.BlockSpec((1, t, LANES), lambda b, p, pt: (b, 0, 0)),
                      pl.BlockSpec((1, page, idx_dim), lambda b, p, pt: (pt[b, p], 0, 0))],
            out_specs=pl.BlockSpec((1, t, page), lambda b, p, pt: (b, 0, p))),
        compiler_params=_params("parallel", "arbitrary"),
        name="sample_idx_scores",
    )(page_table, iq, iw, pool_ik)


def _sample_attn_kernel(pt_ref, lv_ref, g_ref, qa_ref, kan_ref, van_ref, qb_ref, kbn_ref, vbn_ref,
                        iq_ref, iw_ref, ikn_ref, scall_ref, scpage_ref, pak_ref, pav_ref, pbk_ref, pbv_ref,
                        oa_ref, ob_ref, thr_ref, bound_ref, scn_ref, m_ref, l_ref, acc_ref,
                        *, topk, n_idx_heads, n_col_bits, lam_init):
    p_id = pl.program_id(1)
    n_pages = pl.num_programs(1)
    t = qa_ref.shape[1]
    page = pak_ref.shape[1]
    past = scall_ref.shape[2]
    n_a = qa_ref.shape[2] // LANES
    n_bp = qb_ref.shape[2] // LANES
    n_rows = 2 * (n_a + n_bp)
    lane = lax.broadcasted_iota(I32, (t, LANES), 1)
    rowi = lax.broadcasted_iota(I32, (t, page), 0)
    coli = lax.broadcasted_iota(I32, (t, page), 1)

    def pad_rows(x):
        return jnp.concatenate([x, jnp.zeros((page - x.shape[0], x.shape[1]), x.dtype)], axis=0)

    @pl.when(p_id == 0)
    def _():
        m_ref[...] = jnp.full(m_ref.shape, NEG_LOGIT, F32)
        l_ref[...] = jnp.zeros(l_ref.shape, F32)
        acc_ref[...] = jnp.zeros(acc_ref.shape, F32)
        scn = _idx_scores(iq_ref[0], pad_rows(ikn_ref[0]).astype(BF16), iw_ref[0], n_idx_heads)
        scn = jnp.where((coli <= rowi) & (coli < t), scn, MASKED)
        scn_ref[...] = scn
        sc_all = scall_ref[0]
        col_all = lax.broadcasted_iota(I32, (t, past), 1)

        def count(pred_past, pred_new):
            return (jnp.sum(jnp.where(pred_past(sc_all), 1.0, 0.0), axis=1, keepdims=True)
                    + jnp.sum(jnp.where(pred_new(scn), 1.0, 0.0), axis=1, keepdims=True))

        thr, n_ge = _kth_largest(lambda v: count(lambda s: s >= v, lambda s: s >= v), t, topk, past + page)
        thr_ref[...] = thr
        bound_ref[...] = jnp.full((t, 1), INT_MAX, I32)

        @pl.when(jnp.max(n_ge) > topk)
        def _():
            need = topk - count(lambda s: s > thr, lambda s: s > thr)
            bound_ref[...] = _tie_bound(
                lambda j: count(lambda s: (s == thr) & (col_all < j), lambda s: (s == thr) & (past + coli < j)),
                need, n_col_bits, t)

    def attend(ka, va, kb, vb, mask_a, mask_b):
        logits, values = [], []
        for h in range(n_a):
            cols = slice(h * LANES, (h + 1) * LANES)
            q = qa_ref[0, :, cols]
            for c in range(2):
                qc = jnp.where((lane < 64) if c == 0 else (lane >= 64), q, 0.0).astype(BF16)
                logits.append(_dot_nt(qc, ka[:, cols]))
                values.append(va[:, cols])
        for pair in range(n_bp):
            cols = slice(pair * LANES, (pair + 1) * LANES)
            q = qb_ref[0, :, cols]
            for c in range(2):
                qc = jnp.where((lane < 64) if c == 0 else (lane >= 64), q, 0.0).astype(BF16)
                logits.append(_dot_nt(qc, kb[:, cols]))
                values.append(vb[:, cols])
        s = jnp.concatenate(logits, axis=0)
        mask = jnp.concatenate([mask_a.astype(F32)] * (2 * n_a) + [mask_b.astype(F32)] * (2 * n_bp), axis=0) > 0.5
        p, alpha, m_new, l_new = _softmax_step(s, mask, m_ref[...], l_ref[...])
        pb = p.astype(BF16)
        pv = jnp.concatenate([_dot(pb[i * t:(i + 1) * t], values[i]) for i in range(n_rows)], axis=0)
        acc_ref[...] = alpha * acc_ref[...] + pv
        m_ref[...] = m_new
        l_ref[...] = l_new

    thr = thr_ref[...]
    bound = bound_ref[...]
    all_true = coli >= 0
    attend(pak_ref[0].astype(BF16), pav_ref[0].astype(BF16), pbk_ref[0].astype(BF16), pbv_ref[0].astype(BF16),
           all_true, _selected(scpage_ref[0], p_id * page + coli, thr, bound))

    @pl.when(p_id == n_pages - 1)
    def _():
        causal_new = (coli <= rowi) & (coli < t)
        attend(pad_rows(kan_ref[0]).astype(BF16), pad_rows(van_ref[0]).astype(BF16),
               pad_rows(kbn_ref[0]).astype(BF16), pad_rows(vbn_ref[0]).astype(BF16),
               causal_new, _selected(scn_ref[...], past + coli, thr, bound))
        o = acc_ref[...] / l_ref[...]
        lam = _lam(lv_ref, lam_init)
        for h in range(n_a):
            o1 = o[(2 * h) * t:(2 * h + 1) * t]
            o2 = o[(2 * h + 1) * t:(2 * h + 2) * t]
            oa_ref[0, :, h * LANES:(h + 1) * LANES] = _head_norm(o1 - lam * o2, g_ref[...], lam_init)
        for pair in range(n_bp):
            r = 2 * n_a + 2 * pair
            ob_ref[0, :, pair * LANES:(pair + 1) * LANES] = jnp.where(
                lane < 64, o[r * t:(r + 1) * t], o[(r + 1) * t:(r + 2) * t])


def _sample_attn(page_table, lv, subln_g, qa, ka, va, qb, kb, vb, iq, iw, ikp, scores,
                 pool_ak, pool_av, pool_bk, pool_bv, topk, n_idx_heads, lam_init):
    bsz, t, wa = qa.shape
    wb = qb.shape[2]
    n_pages = page_table.shape[1]
    page = pool_ak.shape[1]
    past = n_pages * page
    n_rows = 2 * (wa // LANES + wb // LANES)

    def per_batch(shape):
        return pl.BlockSpec((1,) + tuple(shape[1:]), lambda b, p, pt: (b, 0, 0))

    def paged(width):
        return pl.BlockSpec((1, page, width), lambda b, p, pt: (pt[b, p], 0, 0))

    return pl.pallas_call(
        functools.partial(_sample_attn_kernel, topk=topk, n_idx_heads=n_idx_heads,
                          n_col_bits=max(1, (past + page - 1).bit_length()), lam_init=lam_init),
        out_shape=[jax.ShapeDtypeStruct((bsz, t, wa), F32), jax.ShapeDtypeStruct((bsz, t, wb), F32)],
        grid_spec=pltpu.PrefetchScalarGridSpec(
            num_scalar_prefetch=1,
            grid=(bsz, n_pages),
            in_specs=[pl.BlockSpec(lv.shape, lambda b, p, pt: (0, 0)),
                      pl.BlockSpec(subln_g.shape, lambda b, p, pt: (0, 0)),
                      per_batch(qa.shape), per_batch(ka.shape), per_batch(va.shape),
                      per_batch(qb.shape), per_batch(kb.shape), per_batch(vb.shape),
                      per_batch(iq.shape), per_batch(iw.shape), per_batch(ikp.shape),
                      per_batch(scores.shape),
                      pl.BlockSpec((1, t, page), lambda b, p, pt: (b, 0, p)),
                      paged(wa), paged(wa), paged(wb), paged(wb)],
            out_specs=[per_batch((bsz, t, wa)), per_batch((bsz, t, wb))],
            scratch_shapes=[pltpu.VMEM((t, 1), F32), pltpu.VMEM((t, 1), I32), pltpu.VMEM((t, page), F32),
                            pltpu.VMEM((n_rows * t, 1), F32), pltpu.VMEM((n_rows * t, 1), F32),
                            pltpu.VMEM((n_rows * t, LANES), F32)]),
        compiler_params=_params("parallel", "arbitrary"),
        name="sample_attn",
    )(page_table, lv, subln_g, qa, ka, va, qb, kb, vb, iq, iw, ikp, scores, scores,
      pool_ak, pool_av, pool_bk, pool_bv)


def _outproj_kernel(oa_ref, ob_ref, gates_ref, h_ref, wa_ref, wb_ref, wo_ref, g_ref, b_ref, o_ref, *, alpha):
    ya = _dot(oa_ref[...].astype(BF16), wa_ref[...])
    yb = _dot(ob_ref[...].astype(BF16), wb_ref[...])
    d = ya.shape[1]
    mixed = gates_ref[:, :d] * ya + gates_ref[:, d:] * yb
    o_ref[...] = _layer_norm(alpha * h_ref[...] + _dot(mixed.astype(BF16), wo_ref[...]), g_ref[...], b_ref[...])


def _out_proj(oa, ob, gates, h, wa, wb, wo, g, b, alpha):
    m, d = h.shape
    tm = min(TM_PROJ, m)

    def rows(width):
        return pl.BlockSpec((tm, width), lambda i: (i, 0))

    return pl.pallas_call(
        functools.partial(_outproj_kernel, alpha=alpha),
        out_shape=jax.ShapeDtypeStruct((m, d), F32),
        grid=(m // tm,),
        in_specs=[rows(oa.shape[1]), rows(ob.shape[1]), rows(gates.shape[1]), rows(d),
                  _resident(wa.shape), _resident(wb.shape), _resident(wo.shape),
                  _resident((1, d)), _resident((1, d))],
        out_specs=rows(d),
        compiler_params=_params("parallel"),
        name="out_proj",
    )(oa, ob, gates, h, wa, wb, wo, g, b)


def _rope_tables(pos, dim):
    half = dim // 2
    freqs = ROPE_THETA ** (-jnp.arange(half, dtype=F32) / half)
    ang = pos.astype(F32)[:, None] * freqs[None, :]
    cos, sin = jnp.cos(ang), jnp.sin(ang)
    reps = LANES // dim
    return (jnp.tile(jnp.concatenate([cos, cos], axis=1), (1, reps)),
            jnp.tile(jnp.concatenate([-sin, sin], axis=1), (1, reps)))


def _pack_w_in(w_in, sizes, idx_dim, n_idx_heads):
    d = w_in.shape[0]
    offs = [0]
    for sz in sizes:
        offs.append(offs[-1] + sz)
    part = lambda i: w_in[:, offs[i]:offs[i + 1]]
    pad = lambda w: jnp.pad(w, ((0, 0), (0, LANES - w.shape[1])))
    iq = part(6).reshape(d, n_idx_heads, idx_dim)
    iq = jnp.pad(iq, ((0, 0), (0, 0), (0, LANES - idx_dim))).reshape(d, n_idx_heads * LANES)
    cols = [part(0), part(1), part(2), part(3), part(4), part(5), iq, pad(part(7)), pad(part(8)), part(9)]
    return jnp.concatenate(cols, axis=1).astype(BF16)


def kernel(x_prompt, x_sample, cache_a_k, cache_a_v, cache_b_k, cache_b_v, cache_idx_k, page_table, ln_g, ln_b,
           ffn1_w_gate_up, ffn1_w_down, w_in, b_gate, diff_lambda_vecs, diff_subln_g, w_branch_a, w_branch_b,
           w_o, ffn2_w_gate_up, ffn2_w_down):
    depth = ln_g.shape[0]
    bsz, seq, d_model = x_prompt.shape
    dec_b, dec_t, _ = x_sample.shape
    n_pool, page, ha, _, da = cache_a_k.shape[1:]
    va_dim = cache_a_v.shape[4]
    hb, db = cache_b_k.shape[3:]
    idx_dim = cache_idx_k.shape[3]
    n_pages = page_table.shape[1]
    past = n_pages * page
    d_ff = ffn1_w_down.shape[1]
    w_qa, w_va, w_b = ha * 2 * da, ha * va_dim, hb * db
    n_idx_heads = (w_in.shape[2] - 2 * w_qa - w_va - 3 * w_b - idx_dim - 2 * d_model) // (idx_dim + 1)
    sizes = (w_qa, w_qa, w_va, w_b, w_b, w_b, n_idx_heads * idx_dim, idx_dim, n_idx_heads, 2 * d_model)
    assert sum(sizes) == w_in.shape[2]
    assert 2 * da == LANES and va_dim == LANES and db == 64 and idx_dim == 64 and hb % 2 == 0
    assert da ** -0.5 == db ** -0.5 == idx_dim ** -0.5
    alpha = (2 * depth) ** 0.25
    q_scale = da ** -0.5
    widths = (w_qa, w_va, w_b, w_b, n_idx_heads * LANES, d_model)
    topk_prompt = min(TOPK_MAX, seq // 4)
    topk_sample = min(TOPK_MAX, (past + dec_t) // 4)

    cos_p, sin_p = _rope_tables(jnp.arange(seq), da)
    cos_s, sin_s = _rope_tables(jnp.tile(past + jnp.arange(dec_t), dec_b), da)
    pools = (cache_a_k.reshape(depth, n_pool, page, w_qa), cache_a_v.reshape(depth, n_pool, page, w_va),
             cache_b_k.reshape(depth, n_pool, page, w_b), cache_b_v.reshape(depth, n_pool, page, w_b))

    h_p = x_prompt.reshape(bsz * seq, d_model)
    h_s = x_sample.reshape(dec_b * dec_t, d_model)
    states_p, states_s = [], []
    for layer in range(depth):
        lam_init = _lambda_init(layer)
        row = lambda v: v.reshape(1, -1)
        ffn1 = (ffn1_w_gate_up[layer][:, :d_ff].astype(BF16), ffn1_w_gate_up[layer][:, d_ff:].astype(BF16),
                ffn1_w_down[layer].astype(BF16), row(ln_g[layer, 0]), row(ln_b[layer, 0]))
        ffn2 = (ffn2_w_gate_up[layer][:, :d_ff].astype(BF16), ffn2_w_gate_up[layer][:, d_ff:].astype(BF16),
                ffn2_w_down[layer].astype(BF16), row(ln_g[layer, 2]), row(ln_b[layer, 2]))
        w_all = _pack_w_in(w_in[layer], sizes, idx_dim, n_idx_heads)
        bg = row(b_gate[layer])
        lv = diff_lambda_vecs[layer]
        sg = row(diff_subln_g[layer])
        outw = (w_branch_a[layer].astype(BF16), w_branch_b[layer].astype(BF16), w_o[layer].astype(BF16),
                row(ln_g[layer, 1]), row(ln_b[layer, 1]))

        def mixer_inputs(h1, nb, nt, cos_t, sin_t):
            return _in_proj(h1.reshape(nb, nt, d_model), cos_t, sin_t, w_all, bg, widths, idx_dim,
                            q_scale, n_idx_heads ** -0.5)

        h1 = _ffn_ln(h_p, *ffn1, alpha)
        qa, ka, va, qb, kb, vb, iq, ikp, ik, iw, gates = mixer_inputs(h1, bsz, seq, cos_p, sin_p)
        oa = _diff_attn_prompt(qa, ka, va, lv, sg, lam_init)
        ob = _dsa_prompt(qb, kb, vb, iq, ikp, iw, topk_prompt, n_idx_heads)
        h2 = _out_proj(oa.reshape(-1, w_va), ob.reshape(-1, w_b), gates.reshape(-1, 2 * d_model), h1, *outw, alpha)
        h_p = _ffn_ln(h2, *ffn2, alpha)
        states_p.append((ka, va, kb, vb, ik))

        h1 = _ffn_ln(h_s, *ffn1, alpha)
        outs = mixer_inputs(h1, 1, dec_b * dec_t, cos_s, sin_s)
        qa, ka, va, qb, kb, vb, iq, ikp, ik, iw, gates = [o.reshape(dec_b, dec_t, -1) for o in outs]
        scores = _sample_scores(page_table, iq, iw, cache_idx_k[layer], n_idx_heads)
        oa, ob = _sample_attn(page_table, lv, sg, qa, ka, va, qb, kb, vb, iq, iw, ikp, scores,
                              pools[0][layer], pools[1][layer], pools[2][layer], pools[3][layer],
                              topk_sample, n_idx_heads, lam_init)
        h2 = _out_proj(oa.reshape(-1, w_va), ob.reshape(-1, w_b), gates.reshape(-1, 2 * d_model), h1, *outw, alpha)
        h_s = _ffn_ln(h2, *ffn2, alpha)
        states_s.append((ka, va, kb, vb, ik))

    def stack(states, nb, nt):
        ka, va, kb, vb, ik = [jnp.stack(s) for s in zip(*states)]
        return (ka.reshape(depth, nb, nt, ha, 2, da), va.reshape(depth, nb, nt, ha, va_dim),
                kb.reshape(depth, nb, nt, hb, db), vb.reshape(depth, nb, nt, hb, db),
                ik.reshape(depth, nb, nt, idx_dim))

    return ((h_p.reshape(bsz, seq, d_model), h_s.reshape(dec_b, dec_t, d_model))
            + stack(states_p, bsz, seq) + stack(states_s, dec_b, dec_t))
```

```python
import functools
import math

import numpy as np
import jax
import jax.numpy as jnp
from jax import lax
from jax.experimental import pallas as pl
from jax.experimental.pallas import tpu as pltpu

F32 = jnp.float32
BF16 = jnp.bfloat16
I32 = jnp.int32

LANES = 128
LN_EPS = 1e-5
ROPE_THETA = 10000.0
TOPK_MAX = 256
MASKED = -3.0e38
MASKED_TEST = -1.0e38
NEG_LOGIT = -1.0e30
INT_MAX = 2**31 - 1
INT_MIN = -2**31
VMEM_LIMIT = 56 * 1024 * 1024

TM_FFN = 512
TM_PROJ = 512
F_CHUNK = 1408
T_DIFF = 1024
TQ_DSA = 256
TK_DSA = 512
RQ = 128
CK = 256
ROW_STRIP = 64
PAGES_ATTN = 4
PAGES_IDX = 16


def _lambda_init(layer):
    return 0.8 - 0.6 * math.exp(-0.3 * layer)


def _params(*sem):
    return pltpu.CompilerParams(dimension_semantics=sem, vmem_limit_bytes=VMEM_LIMIT)


def _resident(shape):
    nd = len(shape)
    return pl.BlockSpec(shape, lambda *_: (0,) * nd, pipeline_mode=pl.Buffered(1))


def _dot(a, b):
    return jnp.dot(a, b, preferred_element_type=F32)


def _dot_nt(a, b):
    return lax.dot_general(a, b, (((1,), (1,)), ((), ())), preferred_element_type=F32)


def _layer_norm(x, g, b):
    mu = jnp.mean(x, axis=-1, keepdims=True)
    xc = x - mu
    var = jnp.mean(xc * xc, axis=-1, keepdims=True)
    return xc * lax.rsqrt(var + LN_EPS) * g + b


def _half_mask(shape, c):
    lane = lax.broadcasted_iota(I32, shape, len(shape) - 1)
    return (lane < 64) if c == 0 else (lane >= 64)


def _ffn_kernel(x_ref, wg_ref, wu_ref, wd_ref, g_ref, b_ref, o_ref, *, alpha, f_chunk):
    x = x_ref[...]
    xb = x.astype(BF16)
    d_ff = wg_ref.shape[1]
    acc = jnp.zeros(x.shape, F32)
    for c in range(0, d_ff, f_chunk):
        g = _dot(xb, wg_ref[:, c:c + f_chunk])
        u = _dot(xb, wu_ref[:, c:c + f_chunk])
        hmid = (g / (1.0 + jnp.exp(-g))) * u
        acc = acc + _dot(hmid.astype(BF16), wd_ref[c:c + f_chunk, :])
    o_ref[...] = _layer_norm(alpha * x + 0.5 * acc, g_ref[...], b_ref[...])


def _ffn_ln(x, wg, wu, wd, g, b, alpha):
    m, d = x.shape
    d_ff = wg.shape[1]
    tm = min(TM_FFN, m)
    f_chunk = F_CHUNK if d_ff % F_CHUNK == 0 else d_ff
    return pl.pallas_call(
        functools.partial(_ffn_kernel, alpha=alpha, f_chunk=f_chunk),
        out_shape=jax.ShapeDtypeStruct((m, d), F32),
        grid=(m // tm,),
        in_specs=[pl.BlockSpec((tm, d), lambda i: (i, 0)),
                  _resident((d, d_ff)), _resident((d, d_ff)), _resident((d_ff, d)),
                  _resident((1, d)), _resident((1, d))],
        out_specs=pl.BlockSpec((tm, d), lambda i: (i, 0)),
        compiler_params=_params("parallel"),
        name="ffn_ln",
    )(x, wg, wu, wd, g, b)


def _rope_rows(v, cos, sin):
    lane = lax.broadcasted_iota(I32, v.shape, 1)
    rot = jnp.where((lane % 64) < 32, pltpu.roll(v, LANES - 32, 1), pltpu.roll(v, 32, 1))
    return v * cos + rot * sin


def _rope_cols(v, cos_t, sin_t):
    parts = []
    for r in range(0, v.shape[0], 64):
        parts += [v[r + 32:r + 64], v[r:r + 32]]
    return v * cos_t + jnp.concatenate(parts, axis=0) * sin_t


def _sigmoid(v):
    return 1.0 / (1.0 + jnp.exp(-v))


def _inproj_kernel(h_ref, cos_ref, sin_ref, w_ref, bg_ref,
                   qa_ref, ka_ref, va_ref, qb_ref, kb_ref, vb_ref, iq_ref, ikp_ref, ik_ref, iw_ref, gates_ref,
                   *, widths, q_scale, iw_scale):
    xb = h_ref[0].astype(BF16)
    cos = cos_ref[...]
    sin = sin_ref[...]

    def project(off, width, out_ref, fn):
        v = _dot(xb, w_ref[:, off:off + width])
        for c in range(0, width, LANES):
            out_ref[0, :, c:c + LANES] = fn(v[:, c:c + LANES], c)

    rope_q = lambda v, c: _rope_rows(v, cos, sin) * q_scale
    rope_k = lambda v, c: _rope_rows(v, cos, sin)
    ident = lambda v, c: v
    w_qa, w_va, w_qb, w_vb, w_iq, d_model = widths
    off = 0
    project(off, w_qa, qa_ref, rope_q); off += w_qa
    project(off, w_qa, ka_ref, rope_k); off += w_qa
    project(off, w_va, va_ref, ident); off += w_va
    project(off, w_qb, qb_ref, rope_q); off += w_qb
    project(off, w_qb, kb_ref, rope_k); off += w_qb
    project(off, w_vb, vb_ref, ident); off += w_vb
    project(off, w_iq, iq_ref, rope_q); off += w_iq
    ikp = _rope_rows(_dot(xb, w_ref[:, off:off + LANES]), cos, sin); off += LANES
    ikp_ref[0] = ikp
    ik_ref[0] = ikp[:, :ik_ref.shape[2]]
    iw_ref[0] = _dot(xb, w_ref[:, off:off + LANES]) * iw_scale; off += LANES
    project(off, 2 * d_model, gates_ref, lambda v, c: _sigmoid(v + bg_ref[:, c:c + LANES]))


def _in_proj(h, cos_t, sin_t, w_all, b_gate, widths, idx_dim, q_scale, iw_scale):
    bsz, s, d = h.shape
    w_qa, w_va, w_qb, w_vb, w_iq, d_model = widths
    tm = min(TM_PROJ, s)
    out_w = (w_qa, w_qa, w_va, w_qb, w_qb, w_vb, w_iq, LANES, idx_dim, LANES, 2 * d_model)
    return pl.pallas_call(
        functools.partial(_inproj_kernel, widths=widths, q_scale=q_scale, iw_scale=iw_scale),
        out_shape=[jax.ShapeDtypeStruct((bsz, s, w), F32) for w in out_w],
        grid=(bsz, s // tm),
        in_specs=[pl.BlockSpec((1, tm, d), lambda b, i: (b, i, 0)),
                  pl.BlockSpec((tm, LANES), lambda b, i: (i, 0)),
                  pl.BlockSpec((tm, LANES), lambda b, i: (i, 0)),
                  _resident(w_all.shape), _resident(b_gate.shape)],
        out_specs=[pl.BlockSpec((1, tm, w), lambda b, i: (b, i, 0)) for w in out_w],
        compiler_params=_params("parallel", "parallel"),
        name="in_proj",
    )(h, cos_t, sin_t, w_all, b_gate)


def _inproj_t_kernel(h_ref, cos_ref, sin_ref, cost_ref, sint_ref, w_ref, wt_ref, bg_ref,
                     qa_ref, kat_ref, katb_ref, va4_ref, vab_ref, qb_ref, kbt_ref, kbtb_ref, vbt_ref, vbtb_ref,
                     iq_ref, ikt_ref, iktb_ref, iw_ref, gates_ref, *, widths, idx_dim, q_scale, iw_scale):
    xb = h_ref[0].astype(BF16)
    cos = cos_ref[...]
    sin = sin_ref[...]
    cos_t = cost_ref[...]
    sin_t = sint_ref[...]
    w_qa, w_va, w_qb, w_vb, w_iq, d_model = widths

    off = 0
    v = _dot(xb, w_ref[:, off:off + w_qa]); off += w_qa
    for c in range(0, w_qa, LANES):
        qa_ref[0, :, c:c + LANES] = (_rope_rows(v[:, c:c + LANES], cos, sin) * q_scale).astype(BF16)
    v = _dot(xb, w_ref[:, off:off + w_va]); off += w_va
    vab_ref[0] = v.astype(BF16)
    for hh in range(w_va // LANES):
        va4_ref[0, :, hh, :] = v[:, hh * LANES:(hh + 1) * LANES]
    v = _dot(xb, w_ref[:, off:off + w_qb]); off += w_qb
    for c in range(0, w_qb, LANES):
        qb_ref[0, :, c:c + LANES] = (_rope_rows(v[:, c:c + LANES], cos, sin) * q_scale).astype(BF16)
    v = _dot(xb, w_ref[:, off:off + w_iq]); off += w_iq
    for c in range(0, w_iq, LANES):
        iq_ref[0, :, c:c + LANES] = (_rope_rows(v[:, c:c + LANES], cos, sin) * q_scale).astype(BF16)
    iw_ref[0] = _dot(xb, w_ref[:, off:off + LANES]) * iw_scale; off += LANES
    v = _dot(xb, w_ref[:, off:off + 2 * d_model])
    for c in range(0, 2 * d_model, LANES):
        gates_ref[0, :, c:c + LANES] = _sigmoid(v[:, c:c + LANES] + bg_ref[:, c:c + LANES])

    off = 0
    vt = _dot_nt(wt_ref[off:off + w_qa, :], xb); off += w_qa
    for r in range(0, w_qa, LANES):
        k = _rope_cols(vt[r:r + LANES], cos_t, sin_t)
        kat_ref[0, r:r + LANES, :] = k
        katb_ref[0, r:r + LANES, :] = k.astype(BF16)
    vt = _dot_nt(wt_ref[off:off + w_qb, :], xb); off += w_qb
    for r in range(0, w_qb, LANES):
        k = _rope_cols(vt[r:r + LANES], cos_t, sin_t)
        kbt_ref[0, r:r + LANES, :] = k
        kbtb_ref[0, r:r + LANES, :] = k.astype(BF16)
    vt = _dot_nt(wt_ref[off:off + w_vb, :], xb); off += w_vb
    vbt_ref[0] = vt
    vbtb_ref[0] = vt.astype(BF16)
    k = _rope_cols(_dot_nt(wt_ref[off:off + idx_dim, :], xb), cos_t[:idx_dim], sin_t[:idx_dim])
    ikt_ref[0] = k
    iktb_ref[0] = k.astype(BF16)


def _in_proj_t(h, cos_r, sin_r, cos_c, sin_c, w_rm, w_tr, b_gate, widths, idx_dim, n_heads_a, q_scale, iw_scale):
    bsz, s, d = h.shape
    w_qa, w_va, w_qb, w_vb, w_iq, d_model = widths
    tm = min(TM_PROJ, s)
    rows = lambda w, dt: (jax.ShapeDtypeStruct((bsz, s, w), dt), pl.BlockSpec((1, tm, w), lambda b, i: (b, i, 0)))
    cols = lambda w, dt: (jax.ShapeDtypeStruct((bsz, w, s), dt), pl.BlockSpec((1, w, tm), lambda b, i: (b, 0, i)))
    va4 = (jax.ShapeDtypeStruct((bsz, s, n_heads_a, LANES), F32),
           pl.BlockSpec((1, tm, n_heads_a, LANES), lambda b, i: (b, i, 0, 0)))
    outs = [rows(w_qa, BF16), cols(w_qa, F32), cols(w_qa, BF16), va4, rows(w_va, BF16),
            rows(w_qb, BF16), cols(w_qb, F32), cols(w_qb, BF16), cols(w_vb, F32), cols(w_vb, BF16),
            rows(w_iq, BF16), cols(idx_dim, F32), cols(idx_dim, BF16), rows(LANES, F32), rows(2 * d_model, F32)]
    return pl.pallas_call(
        functools.partial(_inproj_t_kernel, widths=widths, idx_dim=idx_dim, q_scale=q_scale, iw_scale=iw_scale),
        out_shape=[o[0] for o in outs],
        grid=(bsz, s // tm),
        in_specs=[pl.BlockSpec((1, tm, d), lambda b, i: (b, i, 0)),
                  pl.BlockSpec((tm, LANES), lambda b, i: (i, 0)),
                  pl.BlockSpec((tm, LANES), lambda b, i: (i, 0)),
                  pl.BlockSpec((LANES, tm), lambda b, i: (0, i)),
                  pl.BlockSpec((LANES, tm), lambda b, i: (0, i)),
                  _resident(w_rm.shape), _resident(w_tr.shape), _resident(b_gate.shape)],
        out_specs=[o[1] for o in outs],
        compiler_params=_params("parallel", "parallel"),
        name="in_proj_t",
    )(h, cos_r, sin_r, cos_c, sin_c, w_rm, w_tr, b_gate)


def _lam(lv_ref, lam_init):
    lv = lv_ref[...]
    s01 = jnp.sum(lv[0:1] * lv[1:2], axis=1, keepdims=True)
    s23 = jnp.sum(lv[2:3] * lv[3:4], axis=1, keepdims=True)
    return jnp.exp(s01) - jnp.exp(s23) + lam_init


def _head_norm(o, g, lam_init):
    return o * lax.rsqrt(jnp.mean(o * o, axis=-1, keepdims=True) + LN_EPS) * g * (1.0 - lam_init)


def _order_key_to_f32(u):
    bits = jnp.where(u < 0, u ^ INT_MIN, ~u)
    return lax.bitcast_convert_type(bits, F32)


def _kth_largest(count_ge, rows, topk, n_cols):
    def body(it, carry):
        prefix, cnt_at = carry
        cand = prefix | jnp.left_shift(jnp.int32(1), 31 - it)
        cnt = count_ge(_order_key_to_f32(cand))
        ok = cnt >= topk
        return jnp.where(ok, cand, prefix), jnp.where(ok, cnt, cnt_at)

    prefix, cnt_at = lax.fori_loop(
        0, 32, body, (jnp.zeros((rows, 1), I32), jnp.full((rows, 1), n_cols, F32)))
    return _order_key_to_f32(prefix), cnt_at


def _tie_bound(count_eq_below, need, n_bits, rows):
    def body(it, bound):
        cand = bound | jnp.left_shift(jnp.int32(1), n_bits - 1 - it)
        return jnp.where(count_eq_below(cand) <= need - 1.0, cand, bound)

    return lax.fori_loop(0, n_bits, body, jnp.zeros((rows, 1), I32))


def _select_bias(sc, col, thr, bound):
    sel = ((sc > thr) | ((sc == thr) & (col <= bound))) & (sc > MASKED_TEST)
    return jnp.where(sel, 0.0, NEG_LOGIT)


def _softmax_tile(s, m, l, acc, pv):
    m_new = jnp.maximum(m, jnp.max(s, axis=1, keepdims=True))
    p = jnp.exp(s - m_new)
    alpha = jnp.exp(m - m_new)
    return m_new, alpha * l + jnp.sum(p, axis=1, keepdims=True), alpha * acc + pv(p.astype(BF16))


def _causal_steps(n_q, last_of):
    return [(qi, ki) for qi in range(n_q) for ki in range(last_of(qi) + 1)]


def _diffattn_kernel(qi_tab, ki_tab, lv_ref, g_ref, q_ref, kt_ref, v_ref, o_ref, qz_ref, m_ref, l_ref, acc_ref,
                     *, t, lam_init):
    step = pl.program_id(2)
    qi = qi_tab[step]
    ki = ki_tab[step]
    n_ks = t // CK

    @pl.when(ki == 0)
    def _():
        m_ref[...] = jnp.full(m_ref.shape, NEG_LOGIT, F32)
        l_ref[...] = jnp.zeros(l_ref.shape, F32)
        acc_ref[...] = jnp.zeros(acc_ref.shape, F32)
        q = q_ref[0]
        for c in range(2):
            qz_ref[c] = jnp.where(_half_mask(q.shape, c), q, jnp.zeros_like(q))

    def strip(r0, modes, finalize):
        rows = pl.ds(r0, RQ)
        outs = []
        for c in range(2):
            m, l, acc = m_ref[c, rows, :], l_ref[c, rows, :], acc_ref[c, rows, :]
            qz = qz_ref[c, rows, :]
            for ks, mode in enumerate(modes):
                if mode is None:
                    continue
                cols = slice(ks * CK, (ks + 1) * CK)
                s = _dot(qz, kt_ref[0, :, cols])
                if mode == "diag":
                    ri = lax.broadcasted_iota(I32, (RQ, CK), 0)
                    ci = lax.broadcasted_iota(I32, (RQ, CK), 1)
                    s = s + jnp.where(ci + (ks * CK - r0) <= ri, 0.0, NEG_LOGIT)
                m, l, acc = _softmax_tile(s, m, l, acc, lambda p: _dot(p, v_ref[0, cols, :]))
            if finalize:
                outs.append(acc / l)
            else:
                m_ref[c, rows, :], l_ref[c, rows, :], acc_ref[c, rows, :] = m, l, acc
        if finalize:
            o_ref[0, rows, :] = _head_norm(outs[0] - _lam(lv_ref, lam_init) * outs[1], g_ref[...], lam_init)

    @pl.when(ki < qi)
    def _():
        def body(qs, carry):
            strip(pl.multiple_of(qs * RQ, RQ), ["full"] * n_ks, False)
            return carry
        lax.fori_loop(0, t // RQ, body, 0)

    @pl.when(ki == qi)
    def _():
        for r0 in range(0, t, RQ):
            modes = []
            for ks in range(n_ks):
                lo, hi = ks * CK, (ks + 1) * CK - 1
                modes.append(None if lo > r0 + RQ - 1 else ("full" if hi <= r0 else "diag"))
            strip(r0, modes, True)


def _diff_attn_prompt(qa, kat, va, lv, subln_g, lam_init):
    bsz, s, width = qa.shape
    n_heads = width // LANES
    t = min(T_DIFF, s)
    steps = _causal_steps(s // t, lambda qi: qi)
    qi_tab = jnp.asarray(np.array([p[0] for p in steps], np.int32))
    ki_tab = jnp.asarray(np.array([p[1] for p in steps], np.int32))
    return pl.pallas_call(
        functools.partial(_diffattn_kernel, t=t, lam_init=lam_init),
        out_shape=jax.ShapeDtypeStruct((bsz, s, width), F32),
        grid_spec=pltpu.PrefetchScalarGridSpec(
            num_scalar_prefetch=2,
            grid=(bsz, n_heads, len(steps)),
            in_specs=[pl.BlockSpec(lv.shape, lambda b, h, i, qt, kt: (0, 0)),
                      pl.BlockSpec(subln_g.shape, lambda b, h, i, qt, kt: (0, 0)),
                      pl.BlockSpec((1, t, LANES), lambda b, h, i, qt, kt: (b, qt[i], h)),
                      pl.BlockSpec((1, LANES, t), lambda b, h, i, qt, kt: (b, h, kt[i])),
                      pl.BlockSpec((1, t, LANES), lambda b, h, i, qt, kt: (b, kt[i], h))],
            out_specs=pl.BlockSpec((1, t, LANES), lambda b, h, i, qt, kt: (b, qt[i], h)),
            scratch_shapes=[pltpu.VMEM((2, t, LANES), BF16),
                            pltpu.VMEM((2, t, 1), F32), pltpu.VMEM((2, t, 1), F32),
                            pltpu.VMEM((2, t, LANES), F32)]),
        compiler_params=_params("parallel", "parallel", "arbitrary"),
        name="diff_attn_prompt",
    )(qi_tab, ki_tab, lv, subln_g, qa, kat, va)


def _dsa_kernel(qi_tab, ki_tab, ph_tab, ikb_tab, kvb_tab,
                iq_ref, iw_ref, ikt_ref, q_ref, kt_ref, vt_ref, o_ref,
                sc_ref, bias_ref, thr_ref, bound_ref, nge_ref, qz_ref, m_ref, l_ref, acc_ref,
                *, tq, tk, topk, n_idx_heads, idx_dim, n_col_bits):
    step = pl.program_id(1)
    qi = qi_tab[step]
    ki = ki_tab[step]
    phase = ph_tab[step]
    last = ((qi + 1) * tq - 1) // tk
    n_need = last + 1
    n_heads = 2 * (q_ref.shape[2] // LANES)

    @pl.when((phase == 0) & (ki == 0))
    def _():
        m_ref[...] = jnp.full(m_ref.shape, NEG_LOGIT, F32)
        l_ref[...] = jnp.zeros(l_ref.shape, F32)
        acc_ref[...] = jnp.zeros(acc_ref.shape, F32)
        for pair in range(n_heads // 2):
            q = q_ref[0, :, pair * LANES:(pair + 1) * LANES]
            for c in range(2):
                qz_ref[2 * pair + c] = jnp.where(_half_mask(q.shape, c), q, jnp.zeros_like(q))

    @pl.when(phase == 0)
    def _():
        for r0 in range(0, tq, ROW_STRIP):
            rows = slice(r0, r0 + ROW_STRIP)
            iw = iw_ref[0, rows, :]
            for c0 in range(0, tk, CK):
                acc = jnp.zeros((ROW_STRIP, CK), F32)
                for h in range(n_idx_heads):
                    s = _dot(iq_ref[0, rows, h * LANES:h * LANES + idx_dim], ikt_ref[0, :, c0:c0 + CK])
                    acc = acc + jnp.maximum(s, 0.0) * iw[:, h:h + 1]
                row = qi * tq + r0 + lax.broadcasted_iota(I32, (ROW_STRIP, CK), 0)
                col = ki * tk + c0 + lax.broadcasted_iota(I32, (ROW_STRIP, CK), 1)
                sc_ref[ki, rows, c0:c0 + CK] = jnp.where(col <= row, acc, MASKED)

    def count(r0, n_rows, pred):
        def body(c, acc):
            for j in range(0, tk, LANES):
                acc = acc + jnp.where(pred(sc_ref[c, r0:r0 + n_rows, j:j + LANES], c * tk + j), 1.0, 0.0)
            return acc
        acc = lax.fori_loop(0, n_need, body, jnp.zeros((n_rows, LANES), F32))
        return jnp.sum(acc, axis=1, keepdims=True)

    @pl.when((phase == 0) & (ki == last))
    def _():
        for r0 in range(0, tq, ROW_STRIP):
            thr, n_ge = _kth_largest(lambda v: count(r0, ROW_STRIP, lambda sc, c0: sc >= v),
                                     ROW_STRIP, topk, n_need * tk)
            thr_ref[r0:r0 + ROW_STRIP] = thr
            nge_ref[r0:r0 + ROW_STRIP] = n_ge
        bound_ref[...] = jnp.full((tq, 1), INT_MAX, I32)

        @pl.when(jnp.max(nge_ref[...]) > topk)
        def _():
            thr = thr_ref[...]
            lane = lax.broadcasted_iota(I32, (tq, LANES), 1)
            need = topk - count(0, tq, lambda sc, c0: sc > thr)
            bound_ref[...] = _tie_bound(
                lambda j: count(0, tq, lambda sc, c0: (sc == thr) & (c0 + lane < j)), need, n_col_bits, tq)

    @pl.when(phase == 1)
    def _():
        col = ki * tk + lax.broadcasted_iota(I32, (tq, tk), 1)
        bias_ref[...] = _select_bias(sc_ref[ki], col, thr_ref[...], bound_ref[...])
        for r0 in range(0, tq, RQ):
            rows = slice(r0, r0 + RQ)
            for h in range(n_heads):
                feat = slice((h // 2) * LANES, (h // 2 + 1) * LANES)
                m, l, acc = m_ref[h, rows, :], l_ref[h, rows, :], acc_ref[h, rows, :]
                qz = qz_ref[h, rows, :]
                for c0 in range(0, tk, CK):
                    cols = slice(c0, c0 + CK)
                    s = _dot(qz, kt_ref[0, feat, cols]) + bias_ref[rows, cols]
                    m, l, acc = _softmax_tile(s, m, l, acc, lambda p: _dot_nt(p, vt_ref[0, feat, cols]))
                m_ref[h, rows, :], l_ref[h, rows, :], acc_ref[h, rows, :] = m, l, acc

    @pl.when((phase == 1) & (ki == last))
    def _():
        for pair in range(n_heads // 2):
            lo = acc_ref[2 * pair] / l_ref[2 * pair]
            hi = acc_ref[2 * pair + 1] / l_ref[2 * pair + 1]
            o_ref[0, :, pair * LANES:(pair + 1) * LANES] = jnp.where(_half_mask(lo.shape, 0), lo, hi)


def _dsa_prompt(qb, kbt, vbt, iq, ikt, iw, topk, n_idx_heads):
    bsz, s, width = qb.shape
    idx_dim = ikt.shape[1]
    tq, tk = min(TQ_DSA, s), min(TK_DSA, s)
    n_heads = 2 * (width // LANES)
    last_of = lambda qi: ((qi + 1) * tq - 1) // tk
    pairs = _causal_steps(s // tq, last_of)
    tabs = {k: [] for k in ("qi", "ki", "ph", "ikb", "kvb")}
    for qi in range(s // tq):
        mine = [p[1] for p in pairs if p[0] == qi]
        for ph in range(2):
            for ki in mine:
                tabs["qi"].append(qi); tabs["ki"].append(ki); tabs["ph"].append(ph)
                tabs["ikb"].append(ki if ph == 0 else mine[-1])
                tabs["kvb"].append(ki if ph == 1 else 0)
    tab = [jnp.asarray(np.array(tabs[k], np.int32)) for k in ("qi", "ki", "ph", "ikb", "kvb")]

    return pl.pallas_call(
        functools.partial(_dsa_kernel, tq=tq, tk=tk, topk=topk, n_idx_heads=n_idx_heads, idx_dim=idx_dim,
                          n_col_bits=max(1, (s - 1).bit_length())),
        out_shape=jax.ShapeDtypeStruct((bsz, s, width), F32),
        grid_spec=pltpu.PrefetchScalarGridSpec(
            num_scalar_prefetch=5,
            grid=(bsz, len(tabs["qi"])),
            in_specs=[pl.BlockSpec((1, tq, iq.shape[2]), lambda b, i, qt, kt, pt, it, vt: (b, qt[i], 0)),
                      pl.BlockSpec((1, tq, LANES), lambda b, i, qt, kt, pt, it, vt: (b, qt[i], 0)),
                      pl.BlockSpec((1, idx_dim, tk), lambda b, i, qt, kt, pt, it, vt: (b, 0, it[i])),
                      pl.BlockSpec((1, tq, width), lambda b, i, qt, kt, pt, it, vt: (b, qt[i], 0)),
                      pl.BlockSpec((1, width, tk), lambda b, i, qt, kt, pt, it, vt: (b, 0, vt[i])),
                      pl.BlockSpec((1, width, tk), lambda b, i, qt, kt, pt, it, vt: (b, 0, vt[i]))],
            out_specs=pl.BlockSpec((1, tq, width), lambda b, i, qt, kt, pt, it, vt: (b, qt[i], 0)),
            scratch_shapes=[pltpu.VMEM((s // tk, tq, tk), F32), pltpu.VMEM((tq, tk), F32),
                            pltpu.VMEM((tq, 1), F32), pltpu.VMEM((tq, 1), I32), pltpu.VMEM((tq, 1), F32),
                            pltpu.VMEM((n_heads, tq, LANES), BF16),
                            pltpu.VMEM((n_heads, tq, 1), F32), pltpu.VMEM((n_heads, tq, 1), F32),
                            pltpu.VMEM((n_heads, tq, LANES), F32)]),
        compiler_params=_params("parallel", "arbitrary"),
        name="dsa_prompt",
    )(*tab, iq, iw, ikt, qb, kbt, vbt)


def _stack_idx_queries(iq, iw, n_idx_heads, width):
    q = jnp.concatenate([iq[:, h * LANES:h * LANES + width] for h in range(n_idx_heads)], axis=0)
    w = jnp.concatenate([iw[:, h:h + 1] for h in range(n_idx_heads)], axis=0)
    return q.astype(BF16), w


def _sum_heads(x, n_idx_heads, t):
    out = x[0:t]
    for h in range(1, n_idx_heads):
        out = out + x[h * t:(h + 1) * t]
    return out


def _sample_scores_kernel(pt_ref, iq_ref, iw_ref, *refs, n_idx_heads):
    ik_refs, o_ref = refs[:-1], refs[-1]
    t = iq_ref.shape[1]
    page = ik_refs[0].shape[2]
    q, w = _stack_idx_queries(iq_ref[0], iw_ref[0], n_idx_heads, ik_refs[0].shape[1])
    for g, ik_ref in enumerate(ik_refs):
        s = jnp.maximum(_dot(q, ik_ref[0].astype(BF16)), 0.0) * w
        o_ref[0, :, g * page:(g + 1) * page] = _sum_heads(s, n_idx_heads, t)


def _sample_scores(page_table, iq, iw, pool_ikt, n_idx_heads):
    bsz, t, _ = iq.shape
    n_pages = page_table.shape[1]
    idx_dim, page = pool_ikt.shape[1], pool_ikt.shape[2]
    grp = math.gcd(PAGES_IDX, n_pages)

    def paged(g):
        return pl.BlockSpec((1, idx_dim, page), lambda b, p, pt: (pt[b, p * grp + g], 0, 0))

    return pl.pallas_call(
        functools.partial(_sample_scores_kernel, n_idx_heads=n_idx_heads),
        out_shape=jax.ShapeDtypeStruct((bsz, t, n_pages * page), F32),
        grid_spec=pltpu.PrefetchScalarGridSpec(
            num_scalar_prefetch=1,
            grid=(bsz, n_pages // grp),
            in_specs=[pl.BlockSpec((1, t, iq.shape[2]), lambda b, p, pt: (b, 0, 0)),
                      pl.BlockSpec((1, t, LANES), lambda b, p, pt: (b, 0, 0))] + [paged(g) for g in range(grp)],
            out_specs=pl.BlockSpec((1, t, grp * page), lambda b, p, pt: (b, 0, p))),
        compiler_params=_params("parallel", "arbitrary"),
        name="sample_idx_scores",
    )(page_table, iq, iw, *([pool_ikt] * grp))


def _sample_attn_kernel(pt_ref, lv_ref, g_ref, qa_ref, kan_ref, van_ref, qb_ref, kbn_ref, vbn_ref,
                        iq_ref, iw_ref, ikn_ref, scall_ref, scgrp_ref, *refs,
                        grp, topk, n_idx_heads, n_col_bits, lam_init):
    pak_refs, pav_refs, pbk_refs, pbv_refs = (refs[i * grp:(i + 1) * grp] for i in range(4))
    oa_ref, ob_ref, thr_ref, bound_ref, scn_ref, qz_ref, m_ref, l_ref, acc_ref = refs[4 * grp:]
    p_id = pl.program_id(1)
    n_steps = pl.num_programs(1)
    t = qa_ref.shape[1]
    page = pak_refs[0].shape[2]
    past = scall_ref.shape[2]
    n_a = qa_ref.shape[2] // LANES
    n_bp = qb_ref.shape[2] // LANES
    rowi = lax.broadcasted_iota(I32, (t, page), 0)
    coli = lax.broadcasted_iota(I32, (t, page), 1)
    new_visible = (coli <= rowi) & (coli < t)

    def pad_rows(x):
        return jnp.concatenate([x, jnp.zeros((page - x.shape[0], x.shape[1]), x.dtype)], axis=0)

    def stacked(q):
        return jnp.concatenate([jnp.where(_half_mask(q.shape, c), q, 0.0) for c in range(2)], axis=0).astype(BF16)

    @pl.when(p_id == 0)
    def _():
        m_ref[...] = jnp.full(m_ref.shape, NEG_LOGIT, F32)
        l_ref[...] = jnp.zeros(l_ref.shape, F32)
        acc_ref[...] = jnp.zeros(acc_ref.shape, F32)
        for h in range(n_a):
            qz_ref[h] = stacked(qa_ref[0, :, h * LANES:(h + 1) * LANES])
        for pair in range(n_bp):
            qz_ref[n_a + pair] = stacked(qb_ref[0, :, pair * LANES:(pair + 1) * LANES])
        q, w = _stack_idx_queries(iq_ref[0], iw_ref[0], n_idx_heads, LANES)
        scn = _sum_heads(jnp.maximum(_dot_nt(q, pad_rows(ikn_ref[0]).astype(BF16)), 0.0) * w, n_idx_heads, t)
        scn = jnp.where(new_visible, scn, MASKED)
        scn_ref[...] = scn
        sc_all = scall_ref[0]
        col_all = lax.broadcasted_iota(I32, (t, past), 1)

        def count(pred_past, pred_new):
            return (jnp.sum(jnp.where(pred_past(sc_all), 1.0, 0.0), axis=1, keepdims=True)
                    + jnp.sum(jnp.where(pred_new(scn), 1.0, 0.0), axis=1, keepdims=True))

        thr, n_ge = _kth_largest(lambda v: count(lambda s: s >= v, lambda s: s >= v), t, topk, past + page)
        thr_ref[...] = thr
        bound_ref[...] = jnp.full((t, 1), INT_MAX, I32)

        @pl.when(jnp.max(n_ge) > topk)
        def _():
            need = topk - count(lambda s: s > thr, lambda s: s > thr)
            bound_ref[...] = _tie_bound(
                lambda j: count(lambda s: (s == thr) & (col_all < j), lambda s: (s == thr) & (past + coli < j)),
                need, n_col_bits, t)

    def attend(idx, kt, v_of, bias):
        s = _dot(qz_ref[idx], kt)
        if bias is not None:
            s = s + bias
        m_ref[idx], l_ref[idx], acc_ref[idx] = _softmax_tile(s, m_ref[idx], l_ref[idx], acc_ref[idx], v_of)

    twice = lambda b: jnp.concatenate([b, b], axis=0)
    thr = thr_ref[...]
    bound = bound_ref[...]
    width = grp * page
    colg = p_id * width + lax.broadcasted_iota(I32, (t, width), 1)
    bias_b = twice(_select_bias(scgrp_ref[0], colg, thr, bound))
    for h in range(n_a):
        feat = slice(h * LANES, (h + 1) * LANES)
        kt = jnp.concatenate([r[0, feat, :] for r in pak_refs], axis=1).astype(BF16)
        v = jnp.concatenate([r[0, :, h, :] for r in pav_refs], axis=0).astype(BF16)
        attend(h, kt, lambda p, v=v: _dot(p, v), None)
    for pair in range(n_bp):
        feat = slice(pair * LANES, (pair + 1) * LANES)
        kt = jnp.concatenate([r[0, feat, :] for r in pbk_refs], axis=1).astype(BF16)
        vt = jnp.concatenate([r[0, feat, :] for r in pbv_refs], axis=1).astype(BF16)
        attend(n_a + pair, kt, lambda p, vt=vt: _dot_nt(p, vt), bias_b)

    @pl.when(p_id == n_steps - 1)
    def _():
        bias_a = twice(jnp.where(new_visible, 0.0, NEG_LOGIT))
        bias_n = twice(_select_bias(scn_ref[...], past + coli, thr, bound))
        lam = _lam(lv_ref, lam_init)
        for h in range(n_a):
            feat = slice(h * LANES, (h + 1) * LANES)
            kn = pad_rows(kan_ref[0, :, feat]).astype(BF16)
            vn = pad_rows(van_ref[0, :, feat]).astype(BF16)
            s = _dot_nt(qz_ref[h], kn) + bias_a
            _, l, acc = _softmax_tile(s, m_ref[h], l_ref[h], acc_ref[h], lambda p, vn=vn: _dot(p, vn))
            o = acc / l
            oa_ref[0, :, feat] = _head_norm(o[:t] - lam * o[t:], g_ref[...], lam_init)
        for pair in range(n_bp):
            feat = slice(pair * LANES, (pair + 1) * LANES)
            kn = pad_rows(kbn_ref[0, :, feat]).astype(BF16)
            vn = pad_rows(vbn_ref[0, :, feat]).astype(BF16)
            s = _dot_nt(qz_ref[n_a + pair], kn) + bias_n
            idx = n_a + pair
            _, l, acc = _softmax_tile(s, m_ref[idx], l_ref[idx], acc_ref[idx], lambda p, vn=vn: _dot(p, vn))
            o = acc / l
            ob_ref[0, :, feat] = jnp.where(_half_mask((t, LANES), 0), o[:t], o[t:])


def _sample_attn(page_table, lv, subln_g, qa, ka, va, qb, kb, vb, iq, iw, ikp, scores,
                 pool_akt, pool_av, pool_bkt, pool_bvt, topk, n_idx_heads, lam_init):
    bsz, t, wa = qa.shape
    wb = qb.shape[2]
    n_pages = page_table.shape[1]
    page = pool_akt.shape[2]
    past = n_pages * page
    grp = math.gcd(PAGES_ATTN, n_pages)
    n_units = wa // LANES + wb // LANES

    def per_batch(shape):
        return pl.BlockSpec((1,) + tuple(shape[1:]), lambda b, p, pt: (b, 0, 0))

    def paged(shape, g):
        nd = len(shape) - 1
        return pl.BlockSpec((1,) + tuple(shape[1:]), lambda b, p, pt: (pt[b, p * grp + g],) + (0,) * nd)

    pools = [pool_akt, pool_av, pool_bkt, pool_bvt]
    return pl.pallas_call(
        functools.partial(_sample_attn_kernel, grp=grp, topk=topk, n_idx_heads=n_idx_heads,
                          n_col_bits=max(1, (past + page - 1).bit_length()), lam_init=lam_init),
        out_shape=[jax.ShapeDtypeStruct((bsz, t, wa), F32), jax.ShapeDtypeStruct((bsz, t, wb), F32)],
        grid_spec=pltpu.PrefetchScalarGridSpec(
            num_scalar_prefetch=1,
            grid=(bsz, n_pages // grp),
            in_specs=[pl.BlockSpec(lv.shape, lambda b, p, pt: (0, 0)),
                      pl.BlockSpec(subln_g.shape, lambda b, p, pt: (0, 0)),
                      per_batch(qa.shape), per_batch(ka.shape), per_batch(va.shape),
                      per_batch(qb.shape), per_batch(kb.shape), per_batch(vb.shape),
                      per_batch(iq.shape), per_batch(iw.shape), per_batch(ikp.shape),
                      per_batch(scores.shape),
                      pl.BlockSpec((1, t, grp * page), lambda b, p, pt: (b, 0, p))]
                     + [paged(pool.shape, g) for pool in pools for g in range(grp)],
            out_specs=[per_batch((bsz, t, wa)), per_batch((bsz, t, wb))],
            scratch_shapes=[pltpu.VMEM((t, 1), F32), pltpu.VMEM((t, 1), I32), pltpu.VMEM((t, page), F32),
                            pltpu.VMEM((n_units, 2 * t, LANES), BF16),
                            pltpu.VMEM((n_units, 2 * t, 1), F32), pltpu.VMEM((n_units, 2 * t, 1), F32),
                            pltpu.VMEM((n_units, 2 * t, LANES), F32)]),
        compiler_params=_params("parallel", "arbitrary"),
        name="sample_attn",
    )(page_table, lv, subln_g, qa, ka, va, qb, kb, vb, iq, iw, ikp, scores, scores,
      *[pool for pool in pools for _ in range(grp)])


def _outproj_kernel(oa_ref, ob_ref, gates_ref, h_ref, wa_ref, wb_ref, wo_ref, g_ref, b_ref, o_ref, *, alpha):
    ya = _dot(oa_ref[...].astype(BF16), wa_ref[...])
    yb = _dot(ob_ref[...].astype(BF16), wb_ref[...])
    d = ya.shape[1]
    mixed = gates_ref[:, :d] * ya + gates_ref[:, d:] * yb
    o_ref[...] = _layer_norm(alpha * h_ref[...] + _dot(mixed.astype(BF16), wo_ref[...]), g_ref[...], b_ref[...])


def _out_proj(oa, ob, gates, h, wa, wb, wo, g, b, alpha):
    m, d = h.shape
    tm = min(TM_PROJ, m)

    def rows(width):
        return pl.BlockSpec((tm, width), lambda i: (i, 0))

    return pl.pallas_call(
        functools.partial(_outproj_kernel, alpha=alpha),
        out_shape=jax.ShapeDtypeStruct((m, d), F32),
        grid=(m // tm,),
        in_specs=[rows(oa.shape[1]), rows(ob.shape[1]), rows(gates.shape[1]), rows(d),
                  _resident(wa.shape), _resident(wb.shape), _resident(wo.shape),
                  _resident((1, d)), _resident((1, d))],
        out_specs=rows(d),
        compiler_params=_params("parallel"),
        name="out_proj",
    )(oa, ob, gates, h, wa, wb, wo, g, b)


def _rope_tables(pos, dim):
    half = dim // 2
    freqs = ROPE_THETA ** (-jnp.arange(half, dtype=F32) / half)
    ang = pos.astype(F32)[:, None] * freqs[None, :]
    cos, sin = jnp.cos(ang), jnp.sin(ang)
    reps = LANES // dim
    return (jnp.tile(jnp.concatenate([cos, cos], axis=1), (1, reps)),
            jnp.tile(jnp.concatenate([-sin, sin], axis=1), (1, reps)))


def _split_w_in(w_in, sizes):
    offs = [0]
    for sz in sizes:
        offs.append(offs[-1] + sz)
    return [w_in[:, offs[i]:offs[i + 1]] for i in range(len(sizes))]


def _pad_heads(w, n_heads, dim):
    d = w.shape[0]
    return jnp.pad(w.reshape(d, n_heads, dim), ((0, 0), (0, 0), (0, LANES - dim))).reshape(d, n_heads * LANES)


def _pad_lanes(w):
    return jnp.pad(w, ((0, 0), (0, LANES - w.shape[1])))


def kernel(x_prompt, x_sample, cache_a_k, cache_a_v, cache_b_k, cache_b_v, cache_idx_k, page_table, ln_g, ln_b,
           ffn1_w_gate_up, ffn1_w_down, w_in, b_gate, diff_lambda_vecs, diff_subln_g, w_branch_a, w_branch_b,
           w_o, ffn2_w_gate_up, ffn2_w_down):
    depth = ln_g.shape[0]
    bsz, seq, d_model = x_prompt.shape
    dec_b, dec_t, _ = x_sample.shape
    n_pool, page, ha, _, da = cache_a_k.shape[1:]
    va_dim = cache_a_v.shape[4]
    hb, db = cache_b_k.shape[3:]
    idx_dim = cache_idx_k.shape[3]
    n_pages = page_table.shape[1]
    past = n_pages * page
    d_ff = ffn1_w_down.shape[1]
    w_qa, w_va, w_b = ha * 2 * da, ha * va_dim, hb * db
    n_idx_heads = (w_in.shape[2] - 2 * w_qa - w_va - 3 * w_b - idx_dim - 2 * d_model) // (idx_dim + 1)
    sizes = (w_qa, w_qa, w_va, w_b, w_b, w_b, n_idx_heads * idx_dim, idx_dim, n_idx_heads, 2 * d_model)
    assert sum(sizes) == w_in.shape[2]
    assert 2 * da == LANES and va_dim == LANES and db == 64 and idx_dim == 64 and hb % 2 == 0
    assert da ** -0.5 == db ** -0.5 == idx_dim ** -0.5
    alpha = (2 * depth) ** 0.25
    q_scale = da ** -0.5
    iw_scale = n_idx_heads ** -0.5
    widths = (w_qa, w_va, w_b, w_b, n_idx_heads * LANES, d_model)
    topk_prompt = min(TOPK_MAX, seq // 4)
    topk_sample = min(TOPK_MAX, (past + dec_t) // 4)

    cos_p, sin_p = _rope_tables(jnp.arange(seq), da)
    cos_s, sin_s = _rope_tables(jnp.tile(past + jnp.arange(dec_t), dec_b), da)
    pool_akt = jnp.transpose(cache_a_k, (0, 1, 3, 4, 5, 2)).reshape(depth, n_pool, w_qa, page)
    pool_bkt = jnp.transpose(cache_b_k, (0, 1, 3, 4, 2)).reshape(depth, n_pool, w_b, page)
    pool_bvt = jnp.transpose(cache_b_v, (0, 1, 3, 4, 2)).reshape(depth, n_pool, w_b, page)
    pool_ikt = jnp.transpose(cache_idx_k, (0, 1, 3, 2))

    h_p = x_prompt.reshape(bsz * seq, d_model)
    h_s = x_sample.reshape(dec_b * dec_t, d_model)
    states_p, states_s = [], []
    for layer in range(depth):
        lam_init = _lambda_init(layer)
        row = lambda v: v.reshape(1, -1)
        ffn1 = (ffn1_w_gate_up[layer][:, :d_ff].astype(BF16), ffn1_w_gate_up[layer][:, d_ff:].astype(BF16),
                ffn1_w_down[layer].astype(BF16), row(ln_g[layer, 0]), row(ln_b[layer, 0]))
        ffn2 = (ffn2_w_gate_up[layer][:, :d_ff].astype(BF16), ffn2_w_gate_up[layer][:, d_ff:].astype(BF16),
                ffn2_w_down[layer].astype(BF16), row(ln_g[layer, 2]), row(ln_b[layer, 2]))
        wqa, wka, wva, wqb, wkb, wvb, wiq, wik, wiw, wgl = _split_w_in(w_in[layer], sizes)
        wiq = _pad_heads(wiq, n_idx_heads, idx_dim)
        w_all = jnp.concatenate([wqa, wka, wva, wqb, wkb, wvb, wiq, _pad_lanes(wik), _pad_lanes(wiw), wgl],
                                axis=1).astype(BF16)
        w_rm = jnp.concatenate([wqa, wva, wqb, wiq, _pad_lanes(wiw), wgl], axis=1).astype(BF16)
        w_tr = jnp.concatenate([wka, wkb, wvb, wik], axis=1).T.astype(BF16)
        bg = row(b_gate[layer])
        lv = diff_lambda_vecs[layer]
        sg = row(diff_subln_g[layer])
        outw = (w_branch_a[layer].astype(BF16), w_branch_b[layer].astype(BF16), w_o[layer].astype(BF16),
                row(ln_g[layer, 1]), row(ln_b[layer, 1]))

        h1 = _ffn_ln(h_p, *ffn1, alpha)
        (qa, kat, katb, va4, vab, qb, kbt, kbtb, vbt, vbtb, iq, ikt, iktb, iw, gates) = _in_proj_t(
            h1.reshape(bsz, seq, d_model), cos_p, sin_p, cos_p.T, sin_p.T, w_rm, w_tr, bg, widths, idx_dim, ha,
            q_scale, iw_scale)
        oa = _diff_attn_prompt(qa, katb, vab, lv, sg, lam_init)
        ob = _dsa_prompt(qb, kbtb, vbtb, iq, iktb, iw, topk_prompt, n_idx_heads)
        h2 = _out_proj(oa.reshape(-1, w_va), ob.reshape(-1, w_b), gates.reshape(-1, 2 * d_model), h1, *outw, alpha)
        h_p = _ffn_ln(h2, *ffn2, alpha)
        states_p.append((jnp.transpose(kat.reshape(bsz, ha, 2, da, seq), (0, 4, 1, 2, 3)), va4,
                         jnp.transpose(kbt.reshape(bsz, hb, db, seq), (0, 3, 1, 2)),
                         jnp.transpose(vbt.reshape(bsz, hb, db, seq), (0, 3, 1, 2)),
                         jnp.transpose(ikt, (0, 2, 1))))

        h1 = _ffn_ln(h_s, *ffn1, alpha)
        outs = _in_proj(h1.reshape(1, dec_b * dec_t, d_model), cos_s, sin_s, w_all, bg, widths, idx_dim,
                        q_scale, iw_scale)
        qa, ka, va, qb, kb, vb, iq, ikp, ik, iw, gates = [o.reshape(dec_b, dec_t, -1) for o in outs]
        scores = _sample_scores(page_table, iq, iw, pool_ikt[layer], n_idx_heads)
        oa, ob = _sample_attn(page_table, lv, sg, qa, ka, va, qb, kb, vb, iq, iw, ikp, scores,
                              pool_akt[layer], cache_a_v[layer], pool_bkt[layer], pool_bvt[layer],
                              topk_sample, n_idx_heads, lam_init)
        h2 = _out_proj(oa.reshape(-1, w_va), ob.reshape(-1, w_b), gates.reshape(-1, 2 * d_model), h1, *outw, alpha)
        h_s = _ffn_ln(h2, *ffn2, alpha)
        states_s.append((ka.reshape(dec_b, dec_t, ha, 2, da), va.reshape(dec_b, dec_t, ha, va_dim),
                         kb.reshape(dec_b, dec_t, hb, db), vb.reshape(dec_b, dec_t, hb, db), ik))

    outs = [jnp.stack(s) for s in zip(*states_p)] + [jnp.stack(s) for s in zip(*states_s)]
    return (h_p.reshape(bsz, seq, d_model), h_s.reshape(dec_b, dec_t, d_model), *outs)
```

```python
import functools
import math

import numpy as np
import jax
import jax.numpy as jnp
from jax import lax
from jax.experimental import pallas as pl
from jax.experimental.pallas import tpu as pltpu

F32 = jnp.float32
BF16 = jnp.bfloat16
I32 = jnp.int32

LANES = 128
LN_EPS = 1e-5
ROPE_THETA = 10000.0
TOPK_MAX = 256
MASKED = -3.0e38
MASKED_TEST = -1.0e38
NEG_LOGIT = -1.0e30
INT_MAX = 2**31 - 1
INT_MIN = -2**31
VMEM_LIMIT = 56 * 1024 * 1024

TM_FFN = 512
TM_PROJ = 512
TM_PROJ_T = 256
F_CHUNK = 1408
T_DIFF = 1024
TQ_DSA = 256
TK_DSA = 512
QL = 128
KR = 256
COUNT_ROWS = 64
PAGES_ATTN = 4
PAGES_IDX = 16


def _lambda_init(layer):
    return 0.8 - 0.6 * math.exp(-0.3 * layer)


def _params(*sem):
    return pltpu.CompilerParams(dimension_semantics=sem, vmem_limit_bytes=VMEM_LIMIT)


def _resident(shape):
    nd = len(shape)
    return pl.BlockSpec(shape, lambda *_: (0,) * nd, pipeline_mode=pl.Buffered(1))


def _dot(a, b):
    return jnp.dot(a, b, preferred_element_type=F32)


def _dot_nt(a, b):
    return lax.dot_general(a, b, (((1,), (1,)), ((), ())), preferred_element_type=F32)


def _layer_norm(x, g, b):
    mu = jnp.mean(x, axis=-1, keepdims=True)
    xc = x - mu
    var = jnp.mean(xc * xc, axis=-1, keepdims=True)
    return xc * lax.rsqrt(var + LN_EPS) * g + b


def _half_mask(shape, c, axis):
    idx = lax.broadcasted_iota(I32, shape, axis)
    return (idx < 64) if c == 0 else (idx >= 64)


def _ffn_kernel(x_ref, wg_ref, wu_ref, wd_ref, g_ref, b_ref, o_ref, *, alpha, f_chunk):
    x = x_ref[...]
    xb = x.astype(BF16)
    d_ff = wg_ref.shape[1]
    acc = jnp.zeros(x.shape, F32)
    for c in range(0, d_ff, f_chunk):
        g = _dot(xb, wg_ref[:, c:c + f_chunk])
        u = _dot(xb, wu_ref[:, c:c + f_chunk])
        hmid = (g / (1.0 + jnp.exp(-g))) * u
        acc = acc + _dot(hmid.astype(BF16), wd_ref[c:c + f_chunk, :])
    o_ref[...] = _layer_norm(alpha * x + 0.5 * acc, g_ref[...], b_ref[...])


def _ffn_ln(x, wg, wu, wd, g, b, alpha):
    m, d = x.shape
    d_ff = wg.shape[1]
    tm = min(TM_FFN, m)
    f_chunk = F_CHUNK if d_ff % F_CHUNK == 0 else d_ff
    return pl.pallas_call(
        functools.partial(_ffn_kernel, alpha=alpha, f_chunk=f_chunk),
        out_shape=jax.ShapeDtypeStruct((m, d), F32),
        grid=(m // tm,),
        in_specs=[pl.BlockSpec((tm, d), lambda i: (i, 0)),
                  _resident((d, d_ff)), _resident((d, d_ff)), _resident((d_ff, d)),
                  _resident((1, d)), _resident((1, d))],
        out_specs=pl.BlockSpec((tm, d), lambda i: (i, 0)),
        compiler_params=_params("parallel"),
        name="ffn_ln",
    )(x, wg, wu, wd, g, b)


def _rope_rows(v, cos, sin):
    lane = lax.broadcasted_iota(I32, v.shape, 1)
    rot = jnp.where((lane % 64) < 32, pltpu.roll(v, LANES - 32, 1), pltpu.roll(v, 32, 1))
    return v * cos + rot * sin


def _rope_cols(v, cos_t, sin_t):
    parts = []
    for r in range(0, v.shape[0], 64):
        parts += [v[r + 32:r + 64], v[r:r + 32]]
    return v * cos_t + jnp.concatenate(parts, axis=0) * sin_t


def _sigmoid(v):
    return 1.0 / (1.0 + jnp.exp(-v))


def _inproj_kernel(h_ref, cos_ref, sin_ref, w_ref, bg_ref,
                   qa_ref, ka_ref, va_ref, qb_ref, kb_ref, vb_ref, iq_ref, ikp_ref, ik_ref, iw_ref, gates_ref,
                   *, widths, q_scale, iw_scale):
    xb = h_ref[0].astype(BF16)
    cos = cos_ref[...]
    sin = sin_ref[...]

    def project(off, width, out_ref, fn):
        v = _dot(xb, w_ref[:, off:off + width])
        for c in range(0, width, LANES):
            out_ref[0, :, c:c + LANES] = fn(v[:, c:c + LANES], c)

    rope_q = lambda v, c: _rope_rows(v, cos, sin) * q_scale
    rope_k = lambda v, c: _rope_rows(v, cos, sin)
    ident = lambda v, c: v
    w_qa, w_va, w_qb, w_vb, w_iq, d_model = widths
    off = 0
    project(off, w_qa, qa_ref, rope_q); off += w_qa
    project(off, w_qa, ka_ref, rope_k); off += w_qa
    project(off, w_va, va_ref, ident); off += w_va
    project(off, w_qb, qb_ref, rope_q); off += w_qb
    project(off, w_qb, kb_ref, rope_k); off += w_qb
    project(off, w_vb, vb_ref, ident); off += w_vb
    project(off, w_iq, iq_ref, rope_q); off += w_iq
    ikp = _rope_rows(_dot(xb, w_ref[:, off:off + LANES]), cos, sin); off += LANES
    ikp_ref[0] = ikp
    ik_ref[0] = ikp[:, :ik_ref.shape[2]]
    iw_ref[0] = _dot(xb, w_ref[:, off:off + LANES]) * iw_scale; off += LANES
    project(off, 2 * d_model, gates_ref, lambda v, c: _sigmoid(v + bg_ref[:, c:c + LANES]))


def _in_proj(h, cos_t, sin_t, w_all, b_gate, widths, idx_dim, q_scale, iw_scale):
    bsz, s, d = h.shape
    w_qa, w_va, w_qb, w_vb, w_iq, d_model = widths
    tm = min(TM_PROJ, s)
    out_w = (w_qa, w_qa, w_va, w_qb, w_qb, w_vb, w_iq, LANES, idx_dim, LANES, 2 * d_model)
    return pl.pallas_call(
        functools.partial(_inproj_kernel, widths=widths, q_scale=q_scale, iw_scale=iw_scale),
        out_shape=[jax.ShapeDtypeStruct((bsz, s, w), F32) for w in out_w],
        grid=(bsz, s // tm),
        in_specs=[pl.BlockSpec((1, tm, d), lambda b, i: (b, i, 0)),
                  pl.BlockSpec((tm, LANES), lambda b, i: (i, 0)),
                  pl.BlockSpec((tm, LANES), lambda b, i: (i, 0)),
                  _resident(w_all.shape), _resident(b_gate.shape)],
        out_specs=[pl.BlockSpec((1, tm, w), lambda b, i: (b, i, 0)) for w in out_w],
        compiler_params=_params("parallel", "parallel"),
        name="in_proj",
    )(h, cos_t, sin_t, w_all, b_gate)


def _inproj_t_kernel(h_ref, cos_ref, sin_ref, cost_ref, sint_ref, w_ref, wt_ref, bg_ref,
                     ka_ref, va4_ref, kb_ref, ik_ref, gates_ref,
                     qat_ref, kat_ref, vat_ref, qbt_ref, kbt_ref, vbt_ref, vbtb_ref, iqt_ref, ikt_ref, iwt_ref,
                     *, dims, q_scale, iw_scale):
    xb = h_ref[0].astype(BF16)
    cos = cos_ref[...]
    sin = sin_ref[...]
    cos_t = cost_ref[...]
    sin_t = sint_ref[...]
    w_a, w_b, w_iq, idx_dim, iw_rows, d_model = dims

    off = 0
    v = _dot(xb, w_ref[:, off:off + w_a]); off += w_a
    for c in range(0, w_a, LANES):
        ka_ref[0, :, c:c + LANES] = _rope_rows(v[:, c:c + LANES], cos, sin).astype(BF16)
    v = _dot(xb, w_ref[:, off:off + w_a]); off += w_a
    for hh in range(w_a // LANES):
        va4_ref[0, :, hh, :] = v[:, hh * LANES:(hh + 1) * LANES]
    v = _dot(xb, w_ref[:, off:off + w_b]); off += w_b
    for c in range(0, w_b, LANES):
        kb_ref[0, :, c:c + LANES] = _rope_rows(v[:, c:c + LANES], cos, sin).astype(BF16)
    ik_ref[0] = _rope_rows(_dot(xb, w_ref[:, off:off + LANES]), cos, sin)[:, :idx_dim].astype(BF16); off += LANES
    v = _dot(xb, w_ref[:, off:off + 2 * d_model])
    for c in range(0, 2 * d_model, LANES):
        gates_ref[0, :, c:c + LANES] = _sigmoid(v[:, c:c + LANES] + bg_ref[:, c:c + LANES])

    def project_t(off, width, fn):
        vt = _dot_nt(wt_ref[off:off + width, :], xb)
        for r in range(0, width, LANES):
            fn(r, vt[r:r + LANES])

    def rope_q_to(ref):
        def fn(r, v):
            ref[0, r:r + LANES, :] = (_rope_cols(v, cos_t, sin_t) * q_scale).astype(BF16)
        return fn

    def rope_k_to(ref):
        def fn(r, v):
            ref[0, r:r + LANES, :] = _rope_cols(v, cos_t, sin_t)
        return fn

    def vat_fn(r, v):
        vat_ref[0, r:r + LANES, :] = v.astype(BF16)

    def vbt_fn(r, v):
        vbt_ref[0, r:r + LANES, :] = v
        vbtb_ref[0, r:r + LANES, :] = v.astype(BF16)

    off = 0
    project_t(off, w_a, rope_q_to(qat_ref)); off += w_a
    project_t(off, w_a, rope_k_to(kat_ref)); off += w_a
    project_t(off, w_a, vat_fn); off += w_a
    project_t(off, w_b, rope_q_to(qbt_ref)); off += w_b
    project_t(off, w_b, rope_k_to(kbt_ref)); off += w_b
    project_t(off, w_b, vbt_fn); off += w_b
    project_t(off, w_iq, rope_q_to(iqt_ref)); off += w_iq
    ikt_ref[0] = _rope_cols(_dot_nt(wt_ref[off:off + idx_dim, :], xb), cos_t[:idx_dim], sin_t[:idx_dim])
    off += idx_dim
    iwt_ref[0] = _dot_nt(wt_ref[off:off + iw_rows, :], xb) * iw_scale


def _in_proj_t(h, cos_r, sin_r, cos_c, sin_c, w_rm, w_tr, b_gate, dims, n_heads_a, q_scale, iw_scale):
    bsz, s, d = h.shape
    w_a, w_b, w_iq, idx_dim, iw_rows, d_model = dims
    tm = min(TM_PROJ_T, s)
    rows = lambda w, dt: (jax.ShapeDtypeStruct((bsz, s, w), dt), pl.BlockSpec((1, tm, w), lambda b, i: (b, i, 0)))
    cols = lambda w, dt: (jax.ShapeDtypeStruct((bsz, w, s), dt), pl.BlockSpec((1, w, tm), lambda b, i: (b, 0, i)))
    va4 = (jax.ShapeDtypeStruct((bsz, s, n_heads_a, LANES), F32),
           pl.BlockSpec((1, tm, n_heads_a, LANES), lambda b, i: (b, i, 0, 0)))
    outs = [rows(w_a, BF16), va4, rows(w_b, BF16), rows(idx_dim, BF16), rows(2 * d_model, F32),
            cols(w_a, BF16), cols(w_a, F32), cols(w_a, BF16), cols(w_b, BF16), cols(w_b, F32),
            cols(w_b, F32), cols(w_b, BF16), cols(w_iq, BF16), cols(idx_dim, F32), cols(iw_rows, F32)]
    return pl.pallas_call(
        functools.partial(_inproj_t_kernel, dims=dims, q_scale=q_scale, iw_scale=iw_scale),
        out_shape=[o[0] for o in outs],
        grid=(bsz, s // tm),
        in_specs=[pl.BlockSpec((1, tm, d), lambda b, i: (b, i, 0)),
                  pl.BlockSpec((tm, LANES), lambda b, i: (i, 0)),
                  pl.BlockSpec((tm, LANES), lambda b, i: (i, 0)),
                  pl.BlockSpec((LANES, tm), lambda b, i: (0, i)),
                  pl.BlockSpec((LANES, tm), lambda b, i: (0, i)),
                  _resident(w_rm.shape), _resident(w_tr.shape), _resident(b_gate.shape)],
        out_specs=[o[1] for o in outs],
        compiler_params=_params("parallel", "parallel"),
        name="in_proj_t",
    )(h, cos_r, sin_r, cos_c, sin_c, w_rm, w_tr, b_gate)


def _lam(lv_ref, lam_init):
    lv = lv_ref[...]
    s01 = jnp.sum(lv[0:1] * lv[1:2], axis=1, keepdims=True)
    s23 = jnp.sum(lv[2:3] * lv[3:4], axis=1, keepdims=True)
    return jnp.exp(s01) - jnp.exp(s23) + lam_init


def _head_norm(o, g, lam_init):
    return o * lax.rsqrt(jnp.mean(o * o, axis=-1, keepdims=True) + LN_EPS) * g * (1.0 - lam_init)


def _order_key_to_f32(u):
    bits = jnp.where(u < 0, u ^ INT_MIN, ~u)
    return lax.bitcast_convert_type(bits, F32)


def _kth_largest(count_ge, shape, topk, n_keys):
    def body(it, carry):
        prefix, cnt_at = carry
        cand = prefix | jnp.left_shift(jnp.int32(1), 31 - it)
        cnt = count_ge(_order_key_to_f32(cand))
        ok = cnt >= topk
        return jnp.where(ok, cand, prefix), jnp.where(ok, cnt, cnt_at)

    prefix, cnt_at = lax.fori_loop(0, 32, body, (jnp.zeros(shape, I32), jnp.full(shape, n_keys, F32)))
    return _order_key_to_f32(prefix), cnt_at


def _tie_bound(count_eq_below, need, n_bits, shape):
    def body(it, bound):
        cand = bound | jnp.left_shift(jnp.int32(1), n_bits - 1 - it)
        return jnp.where(count_eq_below(cand) <= need - 1.0, cand, bound)

    return lax.fori_loop(0, n_bits, body, jnp.zeros(shape, I32))


def _select_bias(sc, key, thr, bound):
    sel = ((sc > thr) | ((sc == thr) & (key <= bound))) & (sc > MASKED_TEST)
    return jnp.where(sel, 0.0, NEG_LOGIT)


def _softmax_tile(s, m, l, acc, pv, axis=1):
    m_new = jnp.maximum(m, jnp.max(s, axis=axis, keepdims=True))
    p = jnp.exp(s - m_new)
    alpha = jnp.exp(m - m_new)
    return m_new, alpha * l + jnp.sum(p, axis=axis, keepdims=True), alpha * acc + pv(p.astype(BF16))


def _causal_steps(n_q, last_of):
    return [(qi, ki) for qi in range(n_q) for ki in range(last_of(qi) + 1)]


def _mask_features(q, c):
    return jnp.where(_half_mask(q.shape, c, 0), q, jnp.zeros_like(q))


def _diffattn_kernel(qi_tab, ki_tab, lv_ref, g_ref, qt_ref, k_ref, vt_ref, o_ref, qz_ref, m_ref, l_ref, acc_ref,
                     *, t, lam_init):
    step = pl.program_id(2)
    qi = qi_tab[step]
    ki = ki_tab[step]

    @pl.when(ki == 0)
    def _():
        m_ref[...] = jnp.full(m_ref.shape, NEG_LOGIT, F32)
        l_ref[...] = jnp.zeros(l_ref.shape, F32)
        acc_ref[...] = jnp.zeros(acc_ref.shape, F32)
        for c in range(2):
            qz_ref[c] = _mask_features(qt_ref[0], c)

    def sweep(diag):
        for q0 in range(0, t, QL):
            qs = slice(q0, q0 + QL)
            outs = []
            for c in range(2):
                m, l, acc = m_ref[c, :, qs], l_ref[c, :, qs], acc_ref[c, :, qs]
                qz = qz_ref[c, :, qs]
                for k0 in range(0, t, KR):
                    if diag and k0 > q0 + QL - 1:
                        continue
                    ks = slice(k0, k0 + KR)
                    s = _dot(k_ref[0, ks, :], qz)
                    if diag and k0 + KR - 1 > q0:
                        key = k0 + lax.broadcasted_iota(I32, (KR, QL), 0)
                        qry = q0 + lax.broadcasted_iota(I32, (KR, QL), 1)
                        s = s + jnp.where(key <= qry, 0.0, NEG_LOGIT)
                    m, l, acc = _softmax_tile(s, m, l, acc, lambda p: _dot(vt_ref[0, :, ks], p), axis=0)
                if diag:
                    outs.append(acc / l)
                else:
                    m_ref[c, :, qs], l_ref[c, :, qs], acc_ref[c, :, qs] = m, l, acc
            if diag:
                o = outs[0] - _lam(lv_ref, lam_init) * outs[1]
                o = o * lax.rsqrt(jnp.mean(o * o, axis=0, keepdims=True) + LN_EPS) * g_ref[...] * (1.0 - lam_init)
                o_ref[0, qs, :] = o.T

    @pl.when(ki < qi)
    def _():
        sweep(False)

    @pl.when(ki == qi)
    def _():
        sweep(True)


def _diff_attn_prompt(qat, ka, vat, lv, subln_g, lam_init):
    bsz, width, s = qat.shape
    n_heads = width // LANES
    t = min(T_DIFF, s)
    steps = _causal_steps(s // t, lambda qi: qi)
    qi_tab = jnp.asarray(np.array([p[0] for p in steps], np.int32))
    ki_tab = jnp.asarray(np.array([p[1] for p in steps], np.int32))
    g_rows = jnp.broadcast_to(subln_g.reshape(LANES, 1), (LANES, QL))
    return pl.pallas_call(
        functools.partial(_diffattn_kernel, t=t, lam_init=lam_init),
        out_shape=jax.ShapeDtypeStruct((bsz, s, width), F32),
        grid_spec=pltpu.PrefetchScalarGridSpec(
            num_scalar_prefetch=2,
            grid=(bsz, n_heads, len(steps)),
            in_specs=[pl.BlockSpec(lv.shape, lambda b, h, i, qt, kt: (0, 0)),
                      pl.BlockSpec(g_rows.shape, lambda b, h, i, qt, kt: (0, 0)),
                      pl.BlockSpec((1, LANES, t), lambda b, h, i, qt, kt: (b, h, qt[i])),
                      pl.BlockSpec((1, t, LANES), lambda b, h, i, qt, kt: (b, kt[i], h)),
                      pl.BlockSpec((1, LANES, t), lambda b, h, i, qt, kt: (b, h, kt[i]))],
            out_specs=pl.BlockSpec((1, t, LANES), lambda b, h, i, qt, kt: (b, qt[i], h)),
            scratch_shapes=[pltpu.VMEM((2, LANES, t), BF16),
                            pltpu.VMEM((2, 1, t), F32), pltpu.VMEM((2, 1, t), F32),
                            pltpu.VMEM((2, LANES, t), F32)]),
        compiler_params=_params("parallel", "parallel", "arbitrary"),
        name="diff_attn_prompt",
    )(qi_tab, ki_tab, lv, g_rows, qat, ka, vat)


def _dsa_kernel(qi_tab, ki_tab, ph_tab, ikb_tab, kvb_tab,
                iqt_ref, iwt_ref, ik_ref, qt_ref, k_ref, vt_ref, o_ref,
                sc_ref, bias_ref, thr_ref, bound_ref, nge_ref, qz_ref, m_ref, l_ref, acc_ref,
                *, tq, tk, topk, n_idx_heads, idx_dim, n_key_bits):
    step = pl.program_id(1)
    qi = qi_tab[step]
    ki = ki_tab[step]
    phase = ph_tab[step]
    last = ((qi + 1) * tq - 1) // tk
    n_need = last + 1
    n_heads = 2 * (qt_ref.shape[1] // LANES)

    @pl.when((phase == 0) & (ki == 0))
    def _():
        m_ref[...] = jnp.full(m_ref.shape, NEG_LOGIT, F32)
        l_ref[...] = jnp.zeros(l_ref.shape, F32)
        acc_ref[...] = jnp.zeros(acc_ref.shape, F32)
        for pair in range(n_heads // 2):
            q = qt_ref[0, pair * LANES:(pair + 1) * LANES, :]
            for c in range(2):
                qz_ref[2 * pair + c] = _mask_features(q, c)

    @pl.when(phase == 0)
    def _():
        for q0 in range(0, tq, QL):
            qs = slice(q0, q0 + QL)
            for k0 in range(0, tk, LANES):
                ks = slice(k0, k0 + LANES)
                acc = jnp.zeros((LANES, QL), F32)
                for h in range(n_idx_heads):
                    s = _dot(ik_ref[0, ks, :], iqt_ref[0, h * idx_dim:(h + 1) * idx_dim, qs])
                    acc = acc + jnp.maximum(s, 0.0) * iwt_ref[0, h:h + 1, qs]
                key = ki * tk + k0 + lax.broadcasted_iota(I32, (LANES, QL), 0)
                qry = qi * tq + q0 + lax.broadcasted_iota(I32, (LANES, QL), 1)
                sc_ref[ki, ks, qs] = jnp.where(key <= qry, acc, MASKED)

    def count(qs, pred):
        def body(c, acc):
            for r in range(0, tk, COUNT_ROWS):
                acc = acc + jnp.where(pred(sc_ref[c, r:r + COUNT_ROWS, qs], c * tk + r), 1.0, 0.0)
            return acc
        acc = lax.fori_loop(0, n_need, body, jnp.zeros((COUNT_ROWS, qs.stop - qs.start), F32))
        return jnp.sum(acc, axis=0, keepdims=True)

    @pl.when((phase == 0) & (ki == last))
    def _():
        for q0 in range(0, tq, QL):
            qs = slice(q0, q0 + QL)
            thr, n_ge = _kth_largest(lambda v: count(qs, lambda sc, k0: sc >= v), (1, QL), topk, n_need * tk)
            thr_ref[:, qs] = thr
            nge_ref[:, qs] = n_ge
        bound_ref[...] = jnp.full((1, tq), INT_MAX, I32)

        @pl.when(jnp.max(nge_ref[...]) > topk)
        def _():
            thr = thr_ref[...]
            row = lax.broadcasted_iota(I32, (COUNT_ROWS, tq), 0)
            full = slice(0, tq)
            need = topk - count(full, lambda sc, k0: sc > thr)
            bound_ref[...] = _tie_bound(
                lambda j: count(full, lambda sc, k0: (sc == thr) & (k0 + row < j)), need, n_key_bits, (1, tq))

    @pl.when(phase == 1)
    def _():
        key = ki * tk + lax.broadcasted_iota(I32, (tk, tq), 0)
        bias_ref[...] = _select_bias(sc_ref[ki], key, thr_ref[...], bound_ref[...])
        for q0 in range(0, tq, QL):
            qs = slice(q0, q0 + QL)
            for h in range(n_heads):
                feat = slice((h // 2) * LANES, (h // 2 + 1) * LANES)
                m, l, acc = m_ref[h, :, qs], l_ref[h, :, qs], acc_ref[h, :, qs]
                qz = qz_ref[h, :, qs]
                for k0 in range(0, tk, KR):
                    ks = slice(k0, k0 + KR)
                    s = _dot(k_ref[0, ks, feat], qz) + bias_ref[ks, qs]
                    m, l, acc = _softmax_tile(s, m, l, acc, lambda p: _dot(vt_ref[0, feat, ks], p), axis=0)
                m_ref[h, :, qs], l_ref[h, :, qs], acc_ref[h, :, qs] = m, l, acc

    @pl.when((phase == 1) & (ki == last))
    def _():
        for pair in range(n_heads // 2):
            lo = acc_ref[2 * pair] / l_ref[2 * pair]
            hi = acc_ref[2 * pair + 1] / l_ref[2 * pair + 1]
            o = jnp.where(_half_mask(lo.shape, 0, 0), lo, hi)
            for q0 in range(0, tq, QL):
                o_ref[0, q0:q0 + QL, pair * LANES:(pair + 1) * LANES] = o[:, q0:q0 + QL].T


def _dsa_prompt(qbt, kb, vbt, iqt, ik, iwt, topk, n_idx_heads):
    bsz, width, s = qbt.shape
    idx_dim = ik.shape[2]
    tq, tk = min(TQ_DSA, s), min(TK_DSA, s)
    n_heads = 2 * (width // LANES)
    last_of = lambda qi: ((qi + 1) * tq - 1) // tk
    pairs = _causal_steps(s // tq, last_of)
    tabs = {k: [] for k in ("qi", "ki", "ph", "ikb", "kvb")}
    for qi in range(s // tq):
        mine = [p[1] for p in pairs if p[0] == qi]
        for ph in range(2):
            for ki in mine:
                tabs["qi"].append(qi); tabs["ki"].append(ki); tabs["ph"].append(ph)
                tabs["ikb"].append(ki if ph == 0 else mine[-1])
                tabs["kvb"].append(ki if ph == 1 else 0)
    tab = [jnp.asarray(np.array(tabs[k], np.int32)) for k in ("qi", "ki", "ph", "ikb", "kvb")]

    return pl.pallas_call(
        functools.partial(_dsa_kernel, tq=tq, tk=tk, topk=topk, n_idx_heads=n_idx_heads, idx_dim=idx_dim,
                          n_key_bits=max(1, (s - 1).bit_length())),
        out_shape=jax.ShapeDtypeStruct((bsz, s, width), F32),
        grid_spec=pltpu.PrefetchScalarGridSpec(
            num_scalar_prefetch=5,
            grid=(bsz, len(tabs["qi"])),
            in_specs=[pl.BlockSpec((1, iqt.shape[1], tq), lambda b, i, qt, kt, pt, it, vt: (b, 0, qt[i])),
                      pl.BlockSpec((1, iwt.shape[1], tq), lambda b, i, qt, kt, pt, it, vt: (b, 0, qt[i])),
                      pl.BlockSpec((1, tk, idx_dim), lambda b, i, qt, kt, pt, it, vt: (b, it[i], 0)),
                      pl.BlockSpec((1, width, tq), lambda b, i, qt, kt, pt, it, vt: (b, 0, qt[i])),
                      pl.BlockSpec((1, tk, width), lambda b, i, qt, kt, pt, it, vt: (b, vt[i], 0)),
                      pl.BlockSpec((1, width, tk), lambda b, i, qt, kt, pt, it, vt: (b, 0, vt[i]))],
            out_specs=pl.BlockSpec((1, tq, width), lambda b, i, qt, kt, pt, it, vt: (b, qt[i], 0)),
            scratch_shapes=[pltpu.VMEM((s // tk, tk, tq), F32), pltpu.VMEM((tk, tq), F32),
                            pltpu.VMEM((1, tq), F32), pltpu.VMEM((1, tq), I32), pltpu.VMEM((1, tq), F32),
                            pltpu.VMEM((n_heads, LANES, tq), BF16),
                            pltpu.VMEM((n_heads, 1, tq), F32), pltpu.VMEM((n_heads, 1, tq), F32),
                            pltpu.VMEM((n_heads, LANES, tq), F32)]),
        compiler_params=_params("parallel", "arbitrary"),
        name="dsa_prompt",
    )(*tab, iqt, iwt, ik, qbt, kb, vbt)


def _stack_idx_queries(iq, iw, n_idx_heads, width):
    q = jnp.concatenate([iq[:, h * LANES:h * LANES + width] for h in range(n_idx_heads)], axis=0)
    w = jnp.concatenate([iw[:, h:h + 1] for h in range(n_idx_heads)], axis=0)
    return q.astype(BF16), w


def _sum_heads(x, n_idx_heads, t):
    out = x[0:t]
    for h in range(1, n_idx_heads):
        out = out + x[h * t:(h + 1) * t]
    return out


def _sample_scores_kernel(pt_ref, iq_ref, iw_ref, *refs, n_idx_heads):
    ik_refs, o_ref = refs[:-1], refs[-1]
    t = iq_ref.shape[1]
    page = ik_refs[0].shape[2]
    q, w = _stack_idx_queries(iq_ref[0], iw_ref[0], n_idx_heads, ik_refs[0].shape[1])
    for g, ik_ref in enumerate(ik_refs):
        s = jnp.maximum(_dot(q, ik_ref[0].astype(BF16)), 0.0) * w
        o_ref[0, :, g * page:(g + 1) * page] = _sum_heads(s, n_idx_heads, t)


def _sample_scores(page_table, iq, iw, pool_ikt, n_idx_heads):
    bsz, t, _ = iq.shape
    n_pages = page_table.shape[1]
    idx_dim, page = pool_ikt.shape[1], pool_ikt.shape[2]
    grp = math.gcd(PAGES_IDX, n_pages)

    def paged(g):
        return pl.BlockSpec((1, idx_dim, page), lambda b, p, pt: (pt[b, p * grp + g], 0, 0))

    return pl.pallas_call(
        functools.partial(_sample_scores_kernel, n_idx_heads=n_idx_heads),
        out_shape=jax.ShapeDtypeStruct((bsz, t, n_pages * page), F32),
        grid_spec=pltpu.PrefetchScalarGridSpec(
            num_scalar_prefetch=1,
            grid=(bsz, n_pages // grp),
            in_specs=[pl.BlockSpec((1, t, iq.shape[2]), lambda b, p, pt: (b, 0, 0)),
                      pl.BlockSpec((1, t, LANES), lambda b, p, pt: (b, 0, 0))] + [paged(g) for g in range(grp)],
            out_specs=pl.BlockSpec((1, t, grp * page), lambda b, p, pt: (b, 0, p))),
        compiler_params=_params("parallel", "arbitrary"),
        name="sample_idx_scores",
    )(page_table, iq, iw, *([pool_ikt] * grp))


def _sample_attn_kernel(pt_ref, lv_ref, g_ref, qa_ref, kan_ref, van_ref, qb_ref, kbn_ref, vbn_ref,
                        iq_ref, iw_ref, ikn_ref, scall_ref, scgrp_ref, *refs,
                        grp, topk, n_idx_heads, n_col_bits, lam_init):
    pak_refs, pav_refs, pbk_refs, pbv_refs = (refs[i * grp:(i + 1) * grp] for i in range(4))
    oa_ref, ob_ref, thr_ref, bound_ref, scn_ref, qz_ref, m_ref, l_ref, acc_ref = refs[4 * grp:]
    p_id = pl.program_id(1)
    n_steps = pl.num_programs(1)
    t = qa_ref.shape[1]
    page = pak_refs[0].shape[2]
    past = scall_ref.shape[2]
    n_a = qa_ref.shape[2] // LANES
    n_bp = qb_ref.shape[2] // LANES
    n_units = n_a + n_bp
    rowi = lax.broadcasted_iota(I32, (t, page), 0)
    coli = lax.broadcasted_iota(I32, (t, page), 1)
    new_visible = (coli <= rowi) & (coli < t)

    def pad_rows(x):
        return jnp.concatenate([x, jnp.zeros((page - x.shape[0], x.shape[1]), x.dtype)], axis=0)

    def stacked(q):
        return jnp.concatenate([jnp.where(_half_mask(q.shape, c, 1), q, 0.0) for c in range(2)],
                               axis=0).astype(BF16)

    @pl.when(p_id == 0)
    def _():
        m_ref[...] = jnp.full(m_ref.shape, NEG_LOGIT, F32)
        l_ref[...] = jnp.zeros(l_ref.shape, F32)
        acc_ref[...] = jnp.zeros(acc_ref.shape, F32)
        for h in range(n_a):
            qz_ref[h] = stacked(qa_ref[0, :, h * LANES:(h + 1) * LANES])
        for pair in range(n_bp):
            qz_ref[n_a + pair] = stacked(qb_ref[0, :, pair * LANES:(pair + 1) * LANES])
        q, w = _stack_idx_queries(iq_ref[0], iw_ref[0], n_idx_heads, LANES)
        scn = _sum_heads(jnp.maximum(_dot_nt(q, pad_rows(ikn_ref[0]).astype(BF16)), 0.0) * w, n_idx_heads, t)
        scn = jnp.where(new_visible, scn, MASKED)
        scn_ref[...] = scn
        sc_all = scall_ref[0]
        col_all = lax.broadcasted_iota(I32, (t, past), 1)

        def count(pred_past, pred_new):
            return (jnp.sum(jnp.where(pred_past(sc_all), 1.0, 0.0), axis=1, keepdims=True)
                    + jnp.sum(jnp.where(pred_new(scn), 1.0, 0.0), axis=1, keepdims=True))

        thr, n_ge = _kth_largest(lambda v: count(lambda s: s >= v, lambda s: s >= v), (t, 1), topk, past + page)
        thr_ref[...] = thr
        bound_ref[...] = jnp.full((t, 1), INT_MAX, I32)

        @pl.when(jnp.max(n_ge) > topk)
        def _():
            need = topk - count(lambda s: s > thr, lambda s: s > thr)
            bound_ref[...] = _tie_bound(
                lambda j: count(lambda s: (s == thr) & (col_all < j), lambda s: (s == thr) & (past + coli < j)),
                need, n_col_bits, (t, 1))

    def attend(logits, bias_b, pv_fns):
        s = jnp.concatenate(logits[:n_a] + [x + bias_b for x in logits[n_a:]], axis=0)
        m_new = jnp.maximum(m_ref[...], jnp.max(s, axis=1, keepdims=True))
        p = jnp.exp(s - m_new).astype(BF16)
        alpha = jnp.exp(m_ref[...] - m_new)
        l_new = alpha * l_ref[...] + jnp.sum(p.astype(F32), axis=1, keepdims=True)
        pv = jnp.concatenate([fn(p[u * 2 * t:(u + 1) * 2 * t]) for u, fn in enumerate(pv_fns)], axis=0)
        return m_new, l_new, alpha * acc_ref[...] + pv

    twice = lambda b: jnp.concatenate([b, b], axis=0)
    thr = thr_ref[...]
    bound = bound_ref[...]
    width = grp * page
    colg = p_id * width + lax.broadcasted_iota(I32, (t, width), 1)
    bias_b = twice(_select_bias(scgrp_ref[0], colg, thr, bound))
    logits, pv_fns = [], []
    for h in range(n_a):
        feat = slice(h * LANES, (h + 1) * LANES)
        kt = jnp.concatenate([r[0, feat, :] for r in pak_refs], axis=1).astype(BF16)
        logits.append(_dot(qz_ref[h], kt))
        pv_fns.append(lambda p, h=h: _dot(
            p, jnp.concatenate([r[0, :, h, :] for r in pav_refs], axis=0).astype(BF16)))
    for pair in range(n_bp):
        feat = slice(pair * LANES, (pair + 1) * LANES)
        kt = jnp.concatenate([r[0, feat, :] for r in pbk_refs], axis=1).astype(BF16)
        logits.append(_dot(qz_ref[n_a + pair], kt))
        pv_fns.append(lambda p, feat=feat: _dot_nt(
            p, jnp.concatenate([r[0, feat, :] for r in pbv_refs], axis=1).astype(BF16)))
    m_new, l_new, acc_new = attend(logits, bias_b, pv_fns)
    m_ref[...] = m_new
    l_ref[...] = l_new
    acc_ref[...] = acc_new

    @pl.when(p_id == n_steps - 1)
    def _():
        bias_a = twice(jnp.where(new_visible, 0.0, NEG_LOGIT))
        bias_n = twice(_select_bias(scn_ref[...], past + coli, thr, bound))
        logits, pv_fns = [], []
        for h in range(n_a):
            feat = slice(h * LANES, (h + 1) * LANES)
            logits.append(_dot_nt(qz_ref[h], pad_rows(kan_ref[0, :, feat]).astype(BF16)) + bias_a)
            pv_fns.append(lambda p, feat=feat: _dot(p, pad_rows(van_ref[0, :, feat]).astype(BF16)))
        for pair in range(n_bp):
            feat = slice(pair * LANES, (pair + 1) * LANES)
            logits.append(_dot_nt(qz_ref[n_a + pair], pad_rows(kbn_ref[0, :, feat]).astype(BF16)))
            pv_fns.append(lambda p, feat=feat: _dot(p, pad_rows(vbn_ref[0, :, feat]).astype(BF16)))
        _, l_fin, acc_fin = attend(logits, bias_n, pv_fns)
        o = acc_fin / l_fin
        lam = _lam(lv_ref, lam_init)
        for h in range(n_a):
            r = 2 * t * h
            oa_ref[0, :, h * LANES:(h + 1) * LANES] = _head_norm(
                o[r:r + t] - lam * o[r + t:r + 2 * t], g_ref[...], lam_init)
        for pair in range(n_bp):
            r = 2 * t * (n_a + pair)
            ob_ref[0, :, pair * LANES:(pair + 1) * LANES] = jnp.where(
                _half_mask((t, LANES), 0, 1), o[r:r + t], o[r + t:r + 2 * t])


def _sample_attn(page_table, lv, subln_g, qa, ka, va, qb, kb, vb, iq, iw, ikp, scores,
                 pool_akt, pool_av, pool_bkt, pool_bvt, topk, n_idx_heads, lam_init):
    bsz, t, wa = qa.shape
    wb = qb.shape[2]
    n_pages = page_table.shape[1]
    page = pool_akt.shape[2]
    past = n_pages * page
    grp = math.gcd(PAGES_ATTN, n_pages)
    n_rows = 2 * t * (wa // LANES + wb // LANES)

    def per_batch(shape):
        return pl.BlockSpec((1,) + tuple(shape[1:]), lambda b, p, pt: (b, 0, 0))

    def paged(shape, g):
        nd = len(shape) - 1
        return pl.BlockSpec((1,) + tuple(shape[1:]), lambda b, p, pt: (pt[b, p * grp + g],) + (0,) * nd)

    pools = [pool_akt, pool_av, pool_bkt, pool_bvt]
    return pl.pallas_call(
        functools.partial(_sample_attn_kernel, grp=grp, topk=topk, n_idx_heads=n_idx_heads,
                          n_col_bits=max(1, (past + page - 1).bit_length()), lam_init=lam_init),
        out_shape=[jax.ShapeDtypeStruct((bsz, t, wa), F32), jax.ShapeDtypeStruct((bsz, t, wb), F32)],
        grid_spec=pltpu.PrefetchScalarGridSpec(
            num_scalar_prefetch=1,
            grid=(bsz, n_pages // grp),
            in_specs=[pl.BlockSpec(lv.shape, lambda b, p, pt: (0, 0)),
                      pl.BlockSpec(subln_g.shape, lambda b, p, pt: (0, 0)),
                      per_batch(qa.shape), per_batch(ka.shape), per_batch(va.shape),
                      per_batch(qb.shape), per_batch(kb.shape), per_batch(vb.shape),
                      per_batch(iq.shape), per_batch(iw.shape), per_batch(ikp.shape),
                      per_batch(scores.shape),
                      pl.BlockSpec((1, t, grp * page), lambda b, p, pt: (b, 0, p))]
                     + [paged(pool.shape, g) for pool in pools for g in range(grp)],
            out_specs=[per_batch((bsz, t, wa)), per_batch((bsz, t, wb))],
            scratch_shapes=[pltpu.VMEM((t, 1), F32), pltpu.VMEM((t, 1), I32), pltpu.VMEM((t, page), F32),
                            pltpu.VMEM((n_rows // (2 * t), 2 * t, LANES), BF16),
                            pltpu.VMEM((n_rows, 1), F32), pltpu.VMEM((n_rows, 1), F32),
                            pltpu.VMEM((n_rows, LANES), F32)]),
        compiler_params=_params("parallel", "arbitrary"),
        name="sample_attn",
    )(page_table, lv, subln_g, qa, ka, va, qb, kb, vb, iq, iw, ikp, scores, scores,
      *[pool for pool in pools for _ in range(grp)])


def _outproj_kernel(oa_ref, ob_ref, gates_ref, h_ref, wa_ref, wb_ref, wo_ref, g_ref, b_ref, o_ref, *, alpha):
    ya = _dot(oa_ref[...].astype(BF16), wa_ref[...])
    yb = _dot(ob_ref[...].astype(BF16), wb_ref[...])
    d = ya.shape[1]
    mixed = gates_ref[:, :d] * ya + gates_ref[:, d:] * yb
    o_ref[...] = _layer_norm(alpha * h_ref[...] + _dot(mixed.astype(BF16), wo_ref[...]), g_ref[...], b_ref[...])


def _out_proj(oa, ob, gates, h, wa, wb, wo, g, b, alpha):
    m, d = h.shape
    tm = min(TM_PROJ, m)

    def rows(width):
        return pl.BlockSpec((tm, width), lambda i: (i, 0))

    return pl.pallas_call(
        functools.partial(_outproj_kernel, alpha=alpha),
        out_shape=jax.ShapeDtypeStruct((m, d), F32),
        grid=(m // tm,),
        in_specs=[rows(oa.shape[1]), rows(ob.shape[1]), rows(gates.shape[1]), rows(d),
                  _resident(wa.shape), _resident(wb.shape), _resident(wo.shape),
                  _resident((1, d)), _resident((1, d))],
        out_specs=rows(d),
        compiler_params=_params("parallel"),
        name="out_proj",
    )(oa, ob, gates, h, wa, wb, wo, g, b)


def _rope_tables(pos, dim):
    half = dim // 2
    freqs = ROPE_THETA ** (-jnp.arange(half, dtype=F32) / half)
    ang = pos.astype(F32)[:, None] * freqs[None, :]
    cos, sin = jnp.cos(ang), jnp.sin(ang)
    reps = LANES // dim
    return (jnp.tile(jnp.concatenate([cos, cos], axis=1), (1, reps)),
            jnp.tile(jnp.concatenate([-sin, sin], axis=1), (1, reps)))


def _split_w_in(w_in, sizes):
    offs = [0]
    for sz in sizes:
        offs.append(offs[-1] + sz)
    return [w_in[:, offs[i]:offs[i + 1]] for i in range(len(sizes))]


def _pad_heads(w, n_heads, dim):
    d = w.shape[0]
    return jnp.pad(w.reshape(d, n_heads, dim), ((0, 0), (0, 0), (0, LANES - dim))).reshape(d, n_heads * LANES)


def _pad_cols(w, width):
    return jnp.pad(w, ((0, 0), (0, width - w.shape[1])))


def kernel(x_prompt, x_sample, cache_a_k, cache_a_v, cache_b_k, cache_b_v, cache_idx_k, page_table, ln_g, ln_b,
           ffn1_w_gate_up, ffn1_w_down, w_in, b_gate, diff_lambda_vecs, diff_subln_g, w_branch_a, w_branch_b,
           w_o, ffn2_w_gate_up, ffn2_w_down):
    depth = ln_g.shape[0]
    bsz, seq, d_model = x_prompt.shape
    dec_b, dec_t, _ = x_sample.shape
    n_pool, page, ha, _, da = cache_a_k.shape[1:]
    va_dim = cache_a_v.shape[4]
    hb, db = cache_b_k.shape[3:]
    idx_dim = cache_idx_k.shape[3]
    n_pages = page_table.shape[1]
    past = n_pages * page
    d_ff = ffn1_w_down.shape[1]
    w_qa, w_va, w_b = ha * 2 * da, ha * va_dim, hb * db
    n_idx_heads = (w_in.shape[2] - 2 * w_qa - w_va - 3 * w_b - idx_dim - 2 * d_model) // (idx_dim + 1)
    sizes = (w_qa, w_qa, w_va, w_b, w_b, w_b, n_idx_heads * idx_dim, idx_dim, n_idx_heads, 2 * d_model)
    assert sum(sizes) == w_in.shape[2]
    assert 2 * da == LANES and va_dim == LANES and db == 64 and idx_dim == 64 and hb % 2 == 0
    assert w_qa == w_va and da ** -0.5 == db ** -0.5 == idx_dim ** -0.5
    alpha = (2 * depth) ** 0.25
    q_scale = da ** -0.5
    iw_scale = n_idx_heads ** -0.5
    iw_rows = -(-n_idx_heads // 16) * 16
    widths = (w_qa, w_va, w_b, w_b, n_idx_heads * LANES, d_model)
    dims_t = (w_qa, w_b, n_idx_heads * idx_dim, idx_dim, iw_rows, d_model)
    topk_prompt = min(TOPK_MAX, seq // 4)
    topk_sample = min(TOPK_MAX, (past + dec_t) // 4)

    cos_p, sin_p = _rope_tables(jnp.arange(seq), da)
    cos_s, sin_s = _rope_tables(jnp.tile(past + jnp.arange(dec_t), dec_b), da)
    pool_akt = jnp.transpose(cache_a_k, (0, 1, 3, 4, 5, 2)).reshape(depth, n_pool, w_qa, page)
    pool_bkt = jnp.transpose(cache_b_k, (0, 1, 3, 4, 2)).reshape(depth, n_pool, w_b, page)
    pool_bvt = jnp.transpose(cache_b_v, (0, 1, 3, 4, 2)).reshape(depth, n_pool, w_b, page)
    pool_ikt = jnp.transpose(cache_idx_k, (0, 1, 3, 2))

    h_p = x_prompt.reshape(bsz * seq, d_model)
    h_s = x_sample.reshape(dec_b * dec_t, d_model)
    states_p, states_s = [], []
    for layer in range(depth):
        lam_init = _lambda_init(layer)
        row = lambda v: v.reshape(1, -1)
        ffn1 = (ffn1_w_gate_up[layer][:, :d_ff].astype(BF16), ffn1_w_gate_up[layer][:, d_ff:].astype(BF16),
                ffn1_w_down[layer].astype(BF16), row(ln_g[layer, 0]), row(ln_b[layer, 0]))
        ffn2 = (ffn2_w_gate_up[layer][:, :d_ff].astype(BF16), ffn2_w_gate_up[layer][:, d_ff:].astype(BF16),
                ffn2_w_down[layer].astype(BF16), row(ln_g[layer, 2]), row(ln_b[layer, 2]))
        wqa, wka, wva, wqb, wkb, wvb, wiq, wik, wiw, wgl = _split_w_in(w_in[layer], sizes)
        w_all = jnp.concatenate([wqa, wka, wva, wqb, wkb, wvb, _pad_heads(wiq, n_idx_heads, idx_dim),
                                 _pad_cols(wik, LANES), _pad_cols(wiw, LANES), wgl], axis=1).astype(BF16)
        w_rm = jnp.concatenate([wka, wva, wkb, _pad_cols(wik, LANES), wgl], axis=1).astype(BF16)
        w_tr = jnp.concatenate([wqa, wka, wva, wqb, wkb, wvb, wiq, wik, _pad_cols(wiw, iw_rows)],
                               axis=1).T.astype(BF16)
        bg = row(b_gate[layer])
        lv = diff_lambda_vecs[layer]
        sg = row(diff_subln_g[layer])
        outw = (w_branch_a[layer].astype(BF16), w_branch_b[layer].astype(BF16), w_o[layer].astype(BF16),
                row(ln_g[layer, 1]), row(ln_b[layer, 1]))

        h1 = _ffn_ln(h_p, *ffn1, alpha)
        (ka, va4, kb, ik, gates, qat, kat, vat, qbt, kbt, vbt, vbtb, iqt, ikt, iwt) = _in_proj_t(
            h1.reshape(bsz, seq, d_model), cos_p, sin_p, cos_p.T, sin_p.T, w_rm, w_tr, bg, dims_t, ha,
            q_scale, iw_scale)
        oa = _diff_attn_prompt(qat, ka, vat, lv, sg, lam_init)
        ob = _dsa_prompt(qbt, kb, vbtb, iqt, ik, iwt, topk_prompt, n_idx_heads)
        h2 = _out_proj(oa.reshape(-1, w_va), ob.reshape(-1, w_b), gates.reshape(-1, 2 * d_model), h1, *outw, alpha)
        h_p = _ffn_ln(h2, *ffn2, alpha)
        states_p.append((jnp.transpose(kat.reshape(bsz, ha, 2, da, seq), (0, 4, 1, 2, 3)), va4,
                         jnp.transpose(kbt.reshape(bsz, hb, db, seq), (0, 3, 1, 2)),
                         jnp.transpose(vbt.reshape(bsz, hb, db, seq), (0, 3, 1, 2)),
                         jnp.transpose(ikt, (0, 2, 1))))

        h1 = _ffn_ln(h_s, *ffn1, alpha)
        outs = _in_proj(h1.reshape(1, dec_b * dec_t, d_model), cos_s, sin_s, w_all, bg, widths, idx_dim,
                        q_scale, iw_scale)
        qa, ka, va, qb, kb, vb, iq, ikp, ik, iw, gates = [o.reshape(dec_b, dec_t, -1) for o in outs]
        scores = _sample_scores(page_table, iq, iw, pool_ikt[layer], n_idx_heads)
        oa, ob = _sample_attn(page_table, lv, sg, qa, ka, va, qb, kb, vb, iq, iw, ikp, scores,
                              pool_akt[layer], cache_a_v[layer], pool_bkt[layer], pool_bvt[layer],
                              topk_sample, n_idx_heads, lam_init)
        h2 = _out_proj(oa.reshape(-1, w_va), ob.reshape(-1, w_b), gates.reshape(-1, 2 * d_model), h1, *outw, alpha)
        h_s = _ffn_ln(h2, *ffn2, alpha)
        states_s.append((ka.reshape(dec_b, dec_t, ha, 2, da), va.reshape(dec_b, dec_t, ha, va_dim),
                         kb.reshape(dec_b, dec_t, hb, db), vb.reshape(dec_b, dec_t, hb, db), ik))

    outs = [jnp.stack(s) for s in zip(*states_p)] + [jnp.stack(s) for s in zip(*states_s)]
    return (h_p.reshape(bsz, seq, d_model), h_s.reshape(dec_b, dec_t, d_model), *outs)
```

```python
import functools
import math

import numpy as np
import jax
import jax.numpy as jnp
from jax import lax
from jax.experimental import pallas as pl
from jax.experimental.pallas import tpu as pltpu

F32 = jnp.float32
BF16 = jnp.bfloat16
I32 = jnp.int32

LANES = 128
LN_EPS = 1e-5
ROPE_THETA = 10000.0
TOPK_MAX = 256
MASKED = -3.0e38
MASKED_TEST = -1.0e38
NEG_LOGIT = -1.0e30
LOG2E = math.log2(math.e)
INT_MAX = 2**31 - 1
INT_MIN = -2**31
VMEM_LIMIT = 56 * 1024 * 1024

TM_FFN = 512
TM_PROJ = 512
TM_PROJ_T = 256
F_CHUNK = 1408
T_DIFF = 1024
TQ_DSA = 512
TK_DSA = 512
QL = 128
KR = 256
COUNT_ROWS = 64
PAGES_ATTN = 8
PAGES_IDX = 16


def _lambda_init(layer):
    return 0.8 - 0.6 * math.exp(-0.3 * layer)


def _params(*sem):
    return pltpu.CompilerParams(dimension_semantics=sem, vmem_limit_bytes=VMEM_LIMIT)


def _resident(shape):
    nd = len(shape)
    return pl.BlockSpec(shape, lambda *_: (0,) * nd, pipeline_mode=pl.Buffered(1))


def _dot(a, b):
    return jnp.dot(a, b, preferred_element_type=F32)


def _dot_nt(a, b):
    return lax.dot_general(a, b, (((1,), (1,)), ((), ())), preferred_element_type=F32)


def _layer_norm(x, g, b):
    mu = jnp.mean(x, axis=-1, keepdims=True)
    xc = x - mu
    var = jnp.mean(xc * xc, axis=-1, keepdims=True)
    return xc * lax.rsqrt(var + LN_EPS) * g + b


def _half_mask(shape, c, axis):
    idx = lax.broadcasted_iota(I32, shape, axis)
    return (idx < 64) if c == 0 else (idx >= 64)


def _ffn_kernel(x_ref, wg_ref, wu_ref, wd_ref, g_ref, b_ref, o_ref, *, alpha, f_chunk):
    x = x_ref[...]
    xb = x.astype(BF16)
    d_ff = wg_ref.shape[1]
    acc = jnp.zeros(x.shape, F32)
    for c in range(0, d_ff, f_chunk):
        g = _dot(xb, wg_ref[:, c:c + f_chunk])
        u = _dot(xb, wu_ref[:, c:c + f_chunk])
        hmid = (g / (1.0 + jnp.exp(-g))) * u
        acc = acc + _dot(hmid.astype(BF16), wd_ref[c:c + f_chunk, :])
    o_ref[...] = _layer_norm(alpha * x + 0.5 * acc, g_ref[...], b_ref[...])


def _ffn_ln(x, wg, wu, wd, g, b, alpha):
    m, d = x.shape
    d_ff = wg.shape[1]
    tm = min(TM_FFN, m)
    f_chunk = F_CHUNK if d_ff % F_CHUNK == 0 else d_ff
    return pl.pallas_call(
        functools.partial(_ffn_kernel, alpha=alpha, f_chunk=f_chunk),
        out_shape=jax.ShapeDtypeStruct((m, d), F32),
        grid=(m // tm,),
        in_specs=[pl.BlockSpec((tm, d), lambda i: (i, 0)),
                  _resident((d, d_ff)), _resident((d, d_ff)), _resident((d_ff, d)),
                  _resident((1, d)), _resident((1, d))],
        out_specs=pl.BlockSpec((tm, d), lambda i: (i, 0)),
        compiler_params=_params("parallel"),
        name="ffn_ln",
    )(x, wg, wu, wd, g, b)


def _rope_rows(v, cos, sin):
    lane = lax.broadcasted_iota(I32, v.shape, 1)
    rot = jnp.where((lane % 64) < 32, pltpu.roll(v, LANES - 32, 1), pltpu.roll(v, 32, 1))
    return v * cos + rot * sin


def _rope_cols(v, cos_t, sin_t):
    parts = []
    for r in range(0, v.shape[0], 64):
        parts += [v[r + 32:r + 64], v[r:r + 32]]
    return v * cos_t + jnp.concatenate(parts, axis=0) * sin_t


def _sigmoid(v):
    return 1.0 / (1.0 + jnp.exp(-v))


def _inproj_kernel(h_ref, cos_ref, sin_ref, w_ref, bg_ref,
                   qa_ref, ka_ref, va_ref, qb_ref, kb_ref, vb_ref, iq_ref, ikp_ref, ik_ref, iw_ref, gates_ref,
                   *, widths, q_scale, iw_scale):
    xb = h_ref[0].astype(BF16)
    cos = cos_ref[...]
    sin = sin_ref[...]

    def project(off, width, out_ref, fn):
        v = _dot(xb, w_ref[:, off:off + width])
        for c in range(0, width, LANES):
            out_ref[0, :, c:c + LANES] = fn(v[:, c:c + LANES], c)

    attn_scale, idx_scale = q_scale
    rope_q = lambda v, c: _rope_rows(v, cos, sin) * attn_scale
    rope_iq = lambda v, c: _rope_rows(v, cos, sin) * idx_scale
    rope_k = lambda v, c: _rope_rows(v, cos, sin)
    ident = lambda v, c: v
    w_qa, w_va, w_qb, w_vb, w_iq, d_model = widths
    off = 0
    project(off, w_qa, qa_ref, rope_q); off += w_qa
    project(off, w_qa, ka_ref, rope_k); off += w_qa
    project(off, w_va, va_ref, ident); off += w_va
    project(off, w_qb, qb_ref, rope_q); off += w_qb
    project(off, w_qb, kb_ref, rope_k); off += w_qb
    project(off, w_vb, vb_ref, ident); off += w_vb
    project(off, w_iq, iq_ref, rope_iq); off += w_iq
    ikp = _rope_rows(_dot(xb, w_ref[:, off:off + LANES]), cos, sin); off += LANES
    ikp_ref[0] = ikp
    ik_ref[0] = ikp[:, :ik_ref.shape[2]]
    iw_ref[0] = _dot(xb, w_ref[:, off:off + LANES]) * iw_scale; off += LANES
    project(off, 2 * d_model, gates_ref, lambda v, c: _sigmoid(v + bg_ref[:, c:c + LANES]))


def _in_proj(h, cos_t, sin_t, w_all, b_gate, widths, idx_dim, q_scale, iw_scale):
    bsz, s, d = h.shape
    w_qa, w_va, w_qb, w_vb, w_iq, d_model = widths
    tm = min(TM_PROJ, s)
    out_w = (w_qa, w_qa, w_va, w_qb, w_qb, w_vb, w_iq, LANES, idx_dim, LANES, 2 * d_model)
    return pl.pallas_call(
        functools.partial(_inproj_kernel, widths=widths, q_scale=q_scale, iw_scale=iw_scale),
        out_shape=[jax.ShapeDtypeStruct((bsz, s, w), F32) for w in out_w],
        grid=(bsz, s // tm),
        in_specs=[pl.BlockSpec((1, tm, d), lambda b, i: (b, i, 0)),
                  pl.BlockSpec((tm, LANES), lambda b, i: (i, 0)),
                  pl.BlockSpec((tm, LANES), lambda b, i: (i, 0)),
                  _resident(w_all.shape), _resident(b_gate.shape)],
        out_specs=[pl.BlockSpec((1, tm, w), lambda b, i: (b, i, 0)) for w in out_w],
        compiler_params=_params("parallel", "parallel"),
        name="in_proj",
    )(h, cos_t, sin_t, w_all, b_gate)


def _inproj_t_kernel(h_ref, cos_ref, sin_ref, cost_ref, sint_ref, w_ref, wt_ref, bg_ref,
                     ka_ref, va4_ref, kb_ref, ik_ref, gates_ref,
                     qat_ref, kat_ref, vat_ref, qbt_ref, kbt_ref, vbt_ref, vbtb_ref, iqt_ref, ikt_ref, iwt_ref,
                     *, dims, q_scale, iw_scale):
    xb = h_ref[0].astype(BF16)
    cos = cos_ref[...]
    sin = sin_ref[...]
    cos_t = cost_ref[...]
    sin_t = sint_ref[...]
    w_a, w_b, w_iq, idx_dim, iw_rows, d_model = dims

    off = 0
    v = _dot(xb, w_ref[:, off:off + w_a]); off += w_a
    for c in range(0, w_a, LANES):
        ka_ref[0, :, c:c + LANES] = _rope_rows(v[:, c:c + LANES], cos, sin).astype(BF16)
    v = _dot(xb, w_ref[:, off:off + w_a]); off += w_a
    for hh in range(w_a // LANES):
        va4_ref[0, :, hh, :] = v[:, hh * LANES:(hh + 1) * LANES]
    v = _dot(xb, w_ref[:, off:off + w_b]); off += w_b
    for c in range(0, w_b, LANES):
        kb_ref[0, :, c:c + LANES] = _rope_rows(v[:, c:c + LANES], cos, sin).astype(BF16)
    ik_ref[0] = _rope_rows(_dot(xb, w_ref[:, off:off + LANES]), cos, sin)[:, :idx_dim].astype(BF16); off += LANES
    v = _dot(xb, w_ref[:, off:off + 2 * d_model])
    for c in range(0, 2 * d_model, LANES):
        gates_ref[0, :, c:c + LANES] = _sigmoid(v[:, c:c + LANES] + bg_ref[:, c:c + LANES])

    def project_t(off, width, fn):
        vt = _dot_nt(wt_ref[off:off + width, :], xb)
        for r in range(0, width, LANES):
            fn(r, vt[r:r + LANES])

    attn_scale, idx_scale = q_scale

    def rope_q_to(ref, scale):
        def fn(r, v):
            ref[0, r:r + LANES, :] = (_rope_cols(v, cos_t, sin_t) * scale).astype(BF16)
        return fn

    def rope_k_to(ref):
        def fn(r, v):
            ref[0, r:r + LANES, :] = _rope_cols(v, cos_t, sin_t)
        return fn

    def vat_fn(r, v):
        vat_ref[0, r:r + LANES, :] = v.astype(BF16)

    def vbt_fn(r, v):
        vbt_ref[0, r:r + LANES, :] = v
        vbtb_ref[0, r:r + LANES, :] = v.astype(BF16)

    off = 0
    project_t(off, w_a, rope_q_to(qat_ref, attn_scale)); off += w_a
    project_t(off, w_a, rope_k_to(kat_ref)); off += w_a
    project_t(off, w_a, vat_fn); off += w_a
    project_t(off, w_b, rope_q_to(qbt_ref, attn_scale)); off += w_b
    project_t(off, w_b, rope_k_to(kbt_ref)); off += w_b
    project_t(off, w_b, vbt_fn); off += w_b
    project_t(off, w_iq, rope_q_to(iqt_ref, idx_scale)); off += w_iq
    ikt_ref[0] = _rope_cols(_dot_nt(wt_ref[off:off + idx_dim, :], xb), cos_t[:idx_dim], sin_t[:idx_dim])
    off += idx_dim
    iwt_ref[0] = _dot_nt(wt_ref[off:off + iw_rows, :], xb) * iw_scale


def _in_proj_t(h, cos_r, sin_r, cos_c, sin_c, w_rm, w_tr, b_gate, dims, n_heads_a, q_scale, iw_scale):
    bsz, s, d = h.shape
    w_a, w_b, w_iq, idx_dim, iw_rows, d_model = dims
    tm = min(TM_PROJ_T, s)
    rows = lambda w, dt: (jax.ShapeDtypeStruct((bsz, s, w), dt), pl.BlockSpec((1, tm, w), lambda b, i: (b, i, 0)))
    cols = lambda w, dt: (jax.ShapeDtypeStruct((bsz, w, s), dt), pl.BlockSpec((1, w, tm), lambda b, i: (b, 0, i)))
    va4 = (jax.ShapeDtypeStruct((bsz, s, n_heads_a, LANES), F32),
           pl.BlockSpec((1, tm, n_heads_a, LANES), lambda b, i: (b, i, 0, 0)))
    outs = [rows(w_a, BF16), va4, rows(w_b, BF16), rows(idx_dim, BF16), rows(2 * d_model, F32),
            cols(w_a, BF16), cols(w_a, F32), cols(w_a, BF16), cols(w_b, BF16), cols(w_b, F32),
            cols(w_b, F32), cols(w_b, BF16), cols(w_iq, BF16), cols(idx_dim, F32), cols(iw_rows, F32)]
    return pl.pallas_call(
        functools.partial(_inproj_t_kernel, dims=dims, q_scale=q_scale, iw_scale=iw_scale),
        out_shape=[o[0] for o in outs],
        grid=(bsz, s // tm),
        in_specs=[pl.BlockSpec((1, tm, d), lambda b, i: (b, i, 0)),
                  pl.BlockSpec((tm, LANES), lambda b, i: (i, 0)),
                  pl.BlockSpec((tm, LANES), lambda b, i: (i, 0)),
                  pl.BlockSpec((LANES, tm), lambda b, i: (0, i)),
                  pl.BlockSpec((LANES, tm), lambda b, i: (0, i)),
                  _resident(w_rm.shape), _resident(w_tr.shape), _resident(b_gate.shape)],
        out_specs=[o[1] for o in outs],
        compiler_params=_params("parallel", "parallel"),
        name="in_proj_t",
    )(h, cos_r, sin_r, cos_c, sin_c, w_rm, w_tr, b_gate)


def _lam(lv_ref, lam_init):
    lv = lv_ref[...]
    s01 = jnp.sum(lv[0:1] * lv[1:2], axis=1, keepdims=True)
    s23 = jnp.sum(lv[2:3] * lv[3:4], axis=1, keepdims=True)
    return jnp.exp(s01) - jnp.exp(s23) + lam_init


def _head_norm(o, g, lam_init):
    return o * lax.rsqrt(jnp.mean(o * o, axis=-1, keepdims=True) + LN_EPS) * g * (1.0 - lam_init)


def _order_key_to_f32(u):
    bits = jnp.where(u < 0, u ^ INT_MIN, ~u)
    return lax.bitcast_convert_type(bits, F32)


def _kth_largest(count_ge, shape, topk, n_keys):
    def body(it, carry):
        prefix, cnt_at = carry
        cand = prefix | jnp.left_shift(jnp.int32(1), 31 - it)
        cnt = count_ge(_order_key_to_f32(cand))
        ok = cnt >= topk
        return jnp.where(ok, cand, prefix), jnp.where(ok, cnt, cnt_at)

    prefix, cnt_at = lax.fori_loop(0, 32, body, (jnp.zeros(shape, I32), jnp.full(shape, n_keys, F32)))
    return _order_key_to_f32(prefix), cnt_at


def _tie_bound(count_eq_below, need, n_bits, shape):
    def body(it, bound):
        cand = bound | jnp.left_shift(jnp.int32(1), n_bits - 1 - it)
        return jnp.where(count_eq_below(cand) <= need - 1.0, cand, bound)

    return lax.fori_loop(0, n_bits, body, jnp.zeros(shape, I32))


def _select_bias(sc, key, thr, bound):
    sel = ((sc > thr) | ((sc == thr) & (key <= bound))) & (sc > MASKED_TEST)
    return jnp.where(sel, 0.0, NEG_LOGIT)


def _softmax_tile(s, m, l, acc, pv, axis=1):
    m_new = jnp.maximum(m, jnp.max(s, axis=axis, keepdims=True))
    p = jnp.exp2(s - m_new)
    alpha = jnp.exp2(m - m_new)
    return m_new, alpha * l + jnp.sum(p, axis=axis, keepdims=True), alpha * acc + pv(p.astype(BF16))


def _causal_steps(n_q, last_of):
    return [(qi, ki) for qi in range(n_q) for ki in range(last_of(qi) + 1)]


def _mask_features(q, c):
    return jnp.where(_half_mask(q.shape, c, 0), q, jnp.zeros_like(q))


def _diffattn_kernel(qi_tab, ki_tab, lv_ref, g_ref, qt_ref, k_ref, vt_ref, o_ref, qz_ref, m_ref, l_ref, acc_ref,
                     *, t, lam_init):
    step = pl.program_id(2)
    qi = qi_tab[step]
    ki = ki_tab[step]

    @pl.when(ki == 0)
    def _():
        m_ref[...] = jnp.full(m_ref.shape, NEG_LOGIT, F32)
        l_ref[...] = jnp.zeros(l_ref.shape, F32)
        acc_ref[...] = jnp.zeros(acc_ref.shape, F32)
        for c in range(2):
            qz_ref[c] = _mask_features(qt_ref[0], c)

    def sweep(diag):
        for q0 in range(0, t, QL):
            qs = slice(q0, q0 + QL)
            outs = []
            for c in range(2):
                m, l, acc = m_ref[c, :, qs], l_ref[c, :, qs], acc_ref[c, :, qs]
                qz = qz_ref[c, :, qs]
                for k0 in range(0, t, KR):
                    if diag and k0 > q0 + QL - 1:
                        continue
                    ks = slice(k0, k0 + KR)
                    s = _dot(k_ref[0, ks, :], qz)
                    if diag and k0 + KR - 1 > q0:
                        key = k0 + lax.broadcasted_iota(I32, (KR, QL), 0)
                        qry = q0 + lax.broadcasted_iota(I32, (KR, QL), 1)
                        s = s + jnp.where(key <= qry, 0.0, NEG_LOGIT)
                    m, l, acc = _softmax_tile(s, m, l, acc, lambda p: _dot(vt_ref[0, :, ks], p), axis=0)
                if diag:
                    outs.append(acc / l)
                else:
                    m_ref[c, :, qs], l_ref[c, :, qs], acc_ref[c, :, qs] = m, l, acc
            if diag:
                o = outs[0] - _lam(lv_ref, lam_init) * outs[1]
                o = o * lax.rsqrt(jnp.mean(o * o, axis=0, keepdims=True) + LN_EPS) * g_ref[...] * (1.0 - lam_init)
                o_ref[0, qs, :] = o.T

    @pl.when(ki < qi)
    def _():
        sweep(False)

    @pl.when(ki == qi)
    def _():
        sweep(True)


def _diff_attn_prompt(qat, ka, vat, lv, subln_g, lam_init):
    bsz, width, s = qat.shape
    n_heads = width // LANES
    t = min(T_DIFF, s)
    steps = _causal_steps(s // t, lambda qi: qi)
    qi_tab = jnp.asarray(np.array([p[0] for p in steps], np.int32))
    ki_tab = jnp.asarray(np.array([p[1] for p in steps], np.int32))
    g_rows = jnp.broadcast_to(subln_g.reshape(LANES, 1), (LANES, QL))
    return pl.pallas_call(
        functools.partial(_diffattn_kernel, t=t, lam_init=lam_init),
        out_shape=jax.ShapeDtypeStruct((bsz, s, width), F32),
        grid_spec=pltpu.PrefetchScalarGridSpec(
            num_scalar_prefetch=2,
            grid=(bsz, n_heads, len(steps)),
            in_specs=[pl.BlockSpec(lv.shape, lambda b, h, i, qt, kt: (0, 0)),
                      pl.BlockSpec(g_rows.shape, lambda b, h, i, qt, kt: (0, 0)),
                      pl.BlockSpec((1, LANES, t), lambda b, h, i, qt, kt: (b, h, qt[i])),
                      pl.BlockSpec((1, t, LANES), lambda b, h, i, qt, kt: (b, kt[i], h)),
                      pl.BlockSpec((1, LANES, t), lambda b, h, i, qt, kt: (b, h, kt[i]))],
            out_specs=pl.BlockSpec((1, t, LANES), lambda b, h, i, qt, kt: (b, qt[i], h)),
            scratch_shapes=[pltpu.VMEM((2, LANES, t), BF16),
                            pltpu.VMEM((2, 1, t), F32), pltpu.VMEM((2, 1, t), F32),
                            pltpu.VMEM((2, LANES, t), F32)]),
        compiler_params=_params("parallel", "parallel", "arbitrary"),
        name="diff_attn_prompt",
    )(qi_tab, ki_tab, lv, g_rows, qat, ka, vat)


def _dsa_kernel(qi_tab, ki_tab, ph_tab, ikb_tab, kvb_tab,
                iqt_ref, iwt_ref, ik_ref, qt_ref, k_ref, vt_ref, o_ref,
                sc_ref, bias_ref, thr_ref, bound_ref, nge_ref, qz_ref, m_ref, l_ref, acc_ref,
                *, tq, tk, topk, n_idx_heads, idx_dim, n_key_bits):
    step = pl.program_id(1)
    qi = qi_tab[step]
    ki = ki_tab[step]
    phase = ph_tab[step]
    last = ((qi + 1) * tq - 1) // tk
    n_need = last + 1
    n_heads = 2 * (qt_ref.shape[1] // LANES)

    @pl.when((phase == 0) & (ki == 0))
    def _():
        m_ref[...] = jnp.full(m_ref.shape, NEG_LOGIT, F32)
        l_ref[...] = jnp.zeros(l_ref.shape, F32)
        acc_ref[...] = jnp.zeros(acc_ref.shape, F32)
        for pair in range(n_heads // 2):
            q = qt_ref[0, pair * LANES:(pair + 1) * LANES, :]
            for c in range(2):
                qz_ref[2 * pair + c] = _mask_features(q, c)

    @pl.when(phase == 0)
    def _():
        for q0 in range(0, tq, QL):
            qs = slice(q0, q0 + QL)
            for k0 in range(0, tk, LANES):
                ks = slice(k0, k0 + LANES)
                acc = jnp.zeros((LANES, QL), F32)
                for h in range(n_idx_heads):
                    s = _dot(ik_ref[0, ks, :], iqt_ref[0, h * idx_dim:(h + 1) * idx_dim, qs])
                    acc = acc + jnp.maximum(s, 0.0) * iwt_ref[0, h:h + 1, qs]
                key = ki * tk + k0 + lax.broadcasted_iota(I32, (LANES, QL), 0)
                qry = qi * tq + q0 + lax.broadcasted_iota(I32, (LANES, QL), 1)
                sc_ref[ki, ks, qs] = jnp.where(key <= qry, acc, MASKED)

    def count(qs, pred):
        def body(c, acc):
            for r in range(0, tk, COUNT_ROWS):
                acc = acc + jnp.where(pred(sc_ref[c, r:r + COUNT_ROWS, qs], c * tk + r), 1.0, 0.0)
            return acc
        acc = lax.fori_loop(0, n_need, body, jnp.zeros((COUNT_ROWS, qs.stop - qs.start), F32))
        return jnp.sum(acc, axis=0, keepdims=True)

    @pl.when((phase == 0) & (ki == last))
    def _():
        for q0 in range(0, tq, QL):
            qs = slice(q0, q0 + QL)
            thr, n_ge = _kth_largest(lambda v: count(qs, lambda sc, k0: sc >= v), (1, QL), topk, n_need * tk)
            thr_ref[:, qs] = thr
            nge_ref[:, qs] = n_ge
        bound_ref[...] = jnp.full((1, tq), INT_MAX, I32)

        @pl.when(jnp.max(nge_ref[...]) > topk)
        def _():
            thr = thr_ref[...]
            row = lax.broadcasted_iota(I32, (COUNT_ROWS, tq), 0)
            full = slice(0, tq)
            need = topk - count(full, lambda sc, k0: sc > thr)
            bound_ref[...] = _tie_bound(
                lambda j: count(full, lambda sc, k0: (sc == thr) & (k0 + row < j)), need, n_key_bits, (1, tq))

    @pl.when(phase == 1)
    def _():
        key = ki * tk + lax.broadcasted_iota(I32, (tk, tq), 0)
        bias_ref[...] = _select_bias(sc_ref[ki], key, thr_ref[...], bound_ref[...])
        for q0 in range(0, tq, QL):
            qs = slice(q0, q0 + QL)
            for h in range(n_heads):
                feat = slice((h // 2) * LANES, (h // 2 + 1) * LANES)
                m, l, acc = m_ref[h, :, qs], l_ref[h, :, qs], acc_ref[h, :, qs]
                qz = qz_ref[h, :, qs]
                for k0 in range(0, tk, KR):
                    ks = slice(k0, k0 + KR)
                    s = _dot(k_ref[0, ks, feat], qz) + bias_ref[ks, qs]
                    m, l, acc = _softmax_tile(s, m, l, acc, lambda p: _dot(vt_ref[0, feat, ks], p), axis=0)
                m_ref[h, :, qs], l_ref[h, :, qs], acc_ref[h, :, qs] = m, l, acc

    @pl.when((phase == 1) & (ki == last))
    def _():
        for pair in range(n_heads // 2):
            lo = acc_ref[2 * pair] / l_ref[2 * pair]
            hi = acc_ref[2 * pair + 1] / l_ref[2 * pair + 1]
            o = jnp.where(_half_mask(lo.shape, 0, 0), lo, hi)
            for q0 in range(0, tq, QL):
                o_ref[0, q0:q0 + QL, pair * LANES:(pair + 1) * LANES] = o[:, q0:q0 + QL].T


def _dsa_prompt(qbt, kb, vbt, iqt, ik, iwt, topk, n_idx_heads):
    bsz, width, s = qbt.shape
    idx_dim = ik.shape[2]
    tq, tk = min(TQ_DSA, s), min(TK_DSA, s)
    n_heads = 2 * (width // LANES)
    last_of = lambda qi: ((qi + 1) * tq - 1) // tk
    pairs = _causal_steps(s // tq, last_of)
    tabs = {k: [] for k in ("qi", "ki", "ph", "ikb", "kvb")}
    for qi in range(s // tq):
        mine = [p[1] for p in pairs if p[0] == qi]
        for ph in range(2):
            for ki in mine:
                tabs["qi"].append(qi); tabs["ki"].append(ki); tabs["ph"].append(ph)
                tabs["ikb"].append(ki if ph == 0 else mine[-1])
                tabs["kvb"].append(ki if ph == 1 else 0)
    tab = [jnp.asarray(np.array(tabs[k], np.int32)) for k in ("qi", "ki", "ph", "ikb", "kvb")]

    return pl.pallas_call(
        functools.partial(_dsa_kernel, tq=tq, tk=tk, topk=topk, n_idx_heads=n_idx_heads, idx_dim=idx_dim,
                          n_key_bits=max(1, (s - 1).bit_length())),
        out_shape=jax.ShapeDtypeStruct((bsz, s, width), F32),
        grid_spec=pltpu.PrefetchScalarGridSpec(
            num_scalar_prefetch=5,
            grid=(bsz, len(tabs["qi"])),
            in_specs=[pl.BlockSpec((1, iqt.shape[1], tq), lambda b, i, qt, kt, pt, it, vt: (b, 0, qt[i])),
                      pl.BlockSpec((1, iwt.shape[1], tq), lambda b, i, qt, kt, pt, it, vt: (b, 0, qt[i])),
                      pl.BlockSpec((1, tk, idx_dim), lambda b, i, qt, kt, pt, it, vt: (b, it[i], 0)),
                      pl.BlockSpec((1, width, tq), lambda b, i, qt, kt, pt, it, vt: (b, 0, qt[i])),
                      pl.BlockSpec((1, tk, width), lambda b, i, qt, kt, pt, it, vt: (b, vt[i], 0)),
                      pl.BlockSpec((1, width, tk), lambda b, i, qt, kt, pt, it, vt: (b, 0, vt[i]))],
            out_specs=pl.BlockSpec((1, tq, width), lambda b, i, qt, kt, pt, it, vt: (b, qt[i], 0)),
            scratch_shapes=[pltpu.VMEM((s // tk, tk, tq), F32), pltpu.VMEM((tk, tq), F32),
                            pltpu.VMEM((1, tq), F32), pltpu.VMEM((1, tq), I32), pltpu.VMEM((1, tq), F32),
                            pltpu.VMEM((n_heads, LANES, tq), BF16),
                            pltpu.VMEM((n_heads, 1, tq), F32), pltpu.VMEM((n_heads, 1, tq), F32),
                            pltpu.VMEM((n_heads, LANES, tq), F32)]),
        compiler_params=_params("parallel", "arbitrary"),
        name="dsa_prompt",
    )(*tab, iqt, iwt, ik, qbt, kb, vbt)


def _stack_idx_queries(iq, iw, n_idx_heads, width):
    q = jnp.concatenate([iq[:, h * LANES:h * LANES + width] for h in range(n_idx_heads)], axis=0)
    w = jnp.concatenate([iw[:, h:h + 1] for h in range(n_idx_heads)], axis=0)
    return q.astype(BF16), w


def _sum_heads(x, n_idx_heads, t):
    out = x[0:t]
    for h in range(1, n_idx_heads):
        out = out + x[h * t:(h + 1) * t]
    return out


def _sample_scores_kernel(pt_ref, iq_ref, iw_ref, *refs, n_idx_heads):
    ik_refs, o_ref = refs[:-1], refs[-1]
    t = iq_ref.shape[1]
    page = ik_refs[0].shape[2]
    q, w = _stack_idx_queries(iq_ref[0], iw_ref[0], n_idx_heads, ik_refs[0].shape[1])
    for g, ik_ref in enumerate(ik_refs):
        s = jnp.maximum(_dot(q, ik_ref[0].astype(BF16)), 0.0) * w
        o_ref[0, :, g * page:(g + 1) * page] = _sum_heads(s, n_idx_heads, t)


def _sample_scores(page_table, iq, iw, pool_ikt, n_idx_heads):
    bsz, t, _ = iq.shape
    n_pages = page_table.shape[1]
    idx_dim, page = pool_ikt.shape[1], pool_ikt.shape[2]
    grp = math.gcd(PAGES_IDX, n_pages)

    def paged(g):
        return pl.BlockSpec((1, idx_dim, page), lambda b, p, pt: (pt[b, p * grp + g], 0, 0))

    return pl.pallas_call(
        functools.partial(_sample_scores_kernel, n_idx_heads=n_idx_heads),
        out_shape=jax.ShapeDtypeStruct((bsz, t, n_pages * page), F32),
        grid_spec=pltpu.PrefetchScalarGridSpec(
            num_scalar_prefetch=1,
            grid=(bsz, n_pages // grp),
            in_specs=[pl.BlockSpec((1, t, iq.shape[2]), lambda b, p, pt: (b, 0, 0)),
                      pl.BlockSpec((1, t, LANES), lambda b, p, pt: (b, 0, 0))] + [paged(g) for g in range(grp)],
            out_specs=pl.BlockSpec((1, t, grp * page), lambda b, p, pt: (b, 0, p))),
        compiler_params=_params("parallel", "arbitrary"),
        name="sample_idx_scores",
    )(page_table, iq, iw, *([pool_ikt] * grp))


def _sample_attn_kernel(pt_ref, lv_ref, g_ref, qa_ref, kan_ref, van_ref, qb_ref, kbn_ref, vbn_ref,
                        iq_ref, iw_ref, ikn_ref, scall_ref, scgrp_ref, *refs,
                        grp, topk, n_idx_heads, n_col_bits, lam_init):
    pak_refs, pav_refs, pbk_refs, pbv_refs = (refs[i * grp:(i + 1) * grp] for i in range(4))
    oa_ref, ob_ref, thr_ref, bound_ref, scn_ref, qz_ref, m_ref, l_ref, acc_ref = refs[4 * grp:]
    p_id = pl.program_id(1)
    n_steps = pl.num_programs(1)
    t = qa_ref.shape[1]
    page = pak_refs[0].shape[2]
    past = scall_ref.shape[2]
    n_a = qa_ref.shape[2] // LANES
    n_bp = qb_ref.shape[2] // LANES
    n_units = n_a + n_bp
    rowi = lax.broadcasted_iota(I32, (t, page), 0)
    coli = lax.broadcasted_iota(I32, (t, page), 1)
    new_visible = (coli <= rowi) & (coli < t)

    def pad_rows(x):
        return jnp.concatenate([x, jnp.zeros((page - x.shape[0], x.shape[1]), x.dtype)], axis=0)

    def stacked(q):
        return jnp.concatenate([jnp.where(_half_mask(q.shape, c, 1), q, 0.0) for c in range(2)],
                               axis=0).astype(BF16)

    @pl.when(p_id == 0)
    def _():
        m_ref[...] = jnp.full(m_ref.shape, NEG_LOGIT, F32)
        l_ref[...] = jnp.zeros(l_ref.shape, F32)
        acc_ref[...] = jnp.zeros(acc_ref.shape, F32)
        for h in range(n_a):
            qz_ref[h] = stacked(qa_ref[0, :, h * LANES:(h + 1) * LANES])
        for pair in range(n_bp):
            qz_ref[n_a + pair] = stacked(qb_ref[0, :, pair * LANES:(pair + 1) * LANES])
        q, w = _stack_idx_queries(iq_ref[0], iw_ref[0], n_idx_heads, LANES)
        scn = _sum_heads(jnp.maximum(_dot_nt(q, pad_rows(ikn_ref[0]).astype(BF16)), 0.0) * w, n_idx_heads, t)
        scn = jnp.where(new_visible, scn, MASKED)
        scn_ref[...] = scn
        sc_all = scall_ref[0]
        col_all = lax.broadcasted_iota(I32, (t, past), 1)

        def count(pred_past, pred_new):
            return (jnp.sum(jnp.where(pred_past(sc_all), 1.0, 0.0), axis=1, keepdims=True)
                    + jnp.sum(jnp.where(pred_new(scn), 1.0, 0.0), axis=1, keepdims=True))

        thr, n_ge = _kth_largest(lambda v: count(lambda s: s >= v, lambda s: s >= v), (t, 1), topk, past + page)
        thr_ref[...] = thr
        bound_ref[...] = jnp.full((t, 1), INT_MAX, I32)

        @pl.when(jnp.max(n_ge) > topk)
        def _():
            need = topk - count(lambda s: s > thr, lambda s: s > thr)
            bound_ref[...] = _tie_bound(
                lambda j: count(lambda s: (s == thr) & (col_all < j), lambda s: (s == thr) & (past + coli < j)),
                need, n_col_bits, (t, 1))

    def attend(logits, bias_b, pv_fns):
        s = jnp.concatenate(logits[:n_a] + [x + bias_b for x in logits[n_a:]], axis=0)
        m_new = jnp.maximum(m_ref[...], jnp.max(s, axis=1, keepdims=True))
        p = jnp.exp2(s - m_new).astype(BF16)
        alpha = jnp.exp2(m_ref[...] - m_new)
        l_new = alpha * l_ref[...] + jnp.sum(p.astype(F32), axis=1, keepdims=True)
        pv = jnp.concatenate([fn(p[u * 2 * t:(u + 1) * 2 * t]) for u, fn in enumerate(pv_fns)], axis=0)
        return m_new, l_new, alpha * acc_ref[...] + pv

    twice = lambda b: jnp.concatenate([b, b], axis=0)
    thr = thr_ref[...]
    bound = bound_ref[...]
    width = grp * page
    colg = p_id * width + lax.broadcasted_iota(I32, (t, width), 1)
    bias_b = twice(_select_bias(scgrp_ref[0], colg, thr, bound))
    logits, pv_fns = [], []
    for h in range(n_a):
        feat = slice(h * LANES, (h + 1) * LANES)
        kt = jnp.concatenate([r[0, feat, :] for r in pak_refs], axis=1).astype(BF16)
        logits.append(_dot(qz_ref[h], kt))
        pv_fns.append(lambda p, h=h: _dot(
            p, jnp.concatenate([r[0, :, h, :] for r in pav_refs], axis=0).astype(BF16)))
    for pair in range(n_bp):
        feat = slice(pair * LANES, (pair + 1) * LANES)
        kt = jnp.concatenate([r[0, feat, :] for r in pbk_refs], axis=1).astype(BF16)
        logits.append(_dot(qz_ref[n_a + pair], kt))
        pv_fns.append(lambda p, feat=feat: _dot_nt(
            p, jnp.concatenate([r[0, feat, :] for r in pbv_refs], axis=1).astype(BF16)))
    m_new, l_new, acc_new = attend(logits, bias_b, pv_fns)
    m_ref[...] = m_new
    l_ref[...] = l_new
    acc_ref[...] = acc_new

    @pl.when(p_id == n_steps - 1)
    def _():
        bias_a = twice(jnp.where(new_visible, 0.0, NEG_LOGIT))
        bias_n = twice(_select_bias(scn_ref[...], past + coli, thr, bound))
        logits, pv_fns = [], []
        for h in range(n_a):
            feat = slice(h * LANES, (h + 1) * LANES)
            logits.append(_dot_nt(qz_ref[h], pad_rows(kan_ref[0, :, feat]).astype(BF16)) + bias_a)
            pv_fns.append(lambda p, feat=feat: _dot(p, pad_rows(van_ref[0, :, feat]).astype(BF16)))
        for pair in range(n_bp):
            feat = slice(pair * LANES, (pair + 1) * LANES)
            logits.append(_dot_nt(qz_ref[n_a + pair], pad_rows(kbn_ref[0, :, feat]).astype(BF16)))
            pv_fns.append(lambda p, feat=feat: _dot(p, pad_rows(vbn_ref[0, :, feat]).astype(BF16)))
        _, l_fin, acc_fin = attend(logits, bias_n, pv_fns)
        o = acc_fin / l_fin
        lam = _lam(lv_ref, lam_init)
        for h in range(n_a):
            r = 2 * t * h
            oa_ref[0, :, h * LANES:(h + 1) * LANES] = _head_norm(
                o[r:r + t] - lam * o[r + t:r + 2 * t], g_ref[...], lam_init)
        for pair in range(n_bp):
            r = 2 * t * (n_a + pair)
            ob_ref[0, :, pair * LANES:(pair + 1) * LANES] = jnp.where(
                _half_mask((t, LANES), 0, 1), o[r:r + t], o[r + t:r + 2 * t])


def _sample_attn(page_table, lv, subln_g, qa, ka, va, qb, kb, vb, iq, iw, ikp, scores,
                 pool_akt, pool_av, pool_bkt, pool_bvt, topk, n_idx_heads, lam_init):
    bsz, t, wa = qa.shape
    wb = qb.shape[2]
    n_pages = page_table.shape[1]
    page = pool_akt.shape[2]
    past = n_pages * page
    grp = math.gcd(PAGES_ATTN, n_pages)
    n_rows = 2 * t * (wa // LANES + wb // LANES)

    def per_batch(shape):
        return pl.BlockSpec((1,) + tuple(shape[1:]), lambda b, p, pt: (b, 0, 0))

    def paged(shape, g):
        nd = len(shape) - 1
        return pl.BlockSpec((1,) + tuple(shape[1:]), lambda b, p, pt: (pt[b, p * grp + g],) + (0,) * nd)

    pools = [pool_akt, pool_av, pool_bkt, pool_bvt]
    return pl.pallas_call(
        functools.partial(_sample_attn_kernel, grp=grp, topk=topk, n_idx_heads=n_idx_heads,
                          n_col_bits=max(1, (past + page - 1).bit_length()), lam_init=lam_init),
        out_shape=[jax.ShapeDtypeStruct((bsz, t, wa), F32), jax.ShapeDtypeStruct((bsz, t, wb), F32)],
        grid_spec=pltpu.PrefetchScalarGridSpec(
            num_scalar_prefetch=1,
            grid=(bsz, n_pages // grp),
            in_specs=[pl.BlockSpec(lv.shape, lambda b, p, pt: (0, 0)),
                      pl.BlockSpec(subln_g.shape, lambda b, p, pt: (0, 0)),
                      per_batch(qa.shape), per_batch(ka.shape), per_batch(va.shape),
                      per_batch(qb.shape), per_batch(kb.shape), per_batch(vb.shape),
                      per_batch(iq.shape), per_batch(iw.shape), per_batch(ikp.shape),
                      per_batch(scores.shape),
                      pl.BlockSpec((1, t, grp * page), lambda b, p, pt: (b, 0, p))]
                     + [paged(pool.shape, g) for pool in pools for g in range(grp)],
            out_specs=[per_batch((bsz, t, wa)), per_batch((bsz, t, wb))],
            scratch_shapes=[pltpu.VMEM((t, 1), F32), pltpu.VMEM((t, 1), I32), pltpu.VMEM((t, page), F32),
                            pltpu.VMEM((n_rows // (2 * t), 2 * t, LANES), BF16),
                            pltpu.VMEM((n_rows, 1), F32), pltpu.VMEM((n_rows, 1), F32),
                            pltpu.VMEM((n_rows, LANES), F32)]),
        compiler_params=_params("parallel", "arbitrary"),
        name="sample_attn",
    )(page_table, lv, subln_g, qa, ka, va, qb, kb, vb, iq, iw, ikp, scores, scores,
      *[pool for pool in pools for _ in range(grp)])


def _outproj_kernel(oa_ref, ob_ref, gates_ref, h_ref, wa_ref, wb_ref, wo_ref, g_ref, b_ref, o_ref, *, alpha):
    ya = _dot(oa_ref[...].astype(BF16), wa_ref[...])
    yb = _dot(ob_ref[...].astype(BF16), wb_ref[...])
    d = ya.shape[1]
    mixed = gates_ref[:, :d] * ya + gates_ref[:, d:] * yb
    o_ref[...] = _layer_norm(alpha * h_ref[...] + _dot(mixed.astype(BF16), wo_ref[...]), g_ref[...], b_ref[...])


def _out_proj(oa, ob, gates, h, wa, wb, wo, g, b, alpha):
    m, d = h.shape
    tm = min(TM_PROJ, m)

    def rows(width):
        return pl.BlockSpec((tm, width), lambda i: (i, 0))

    return pl.pallas_call(
        functools.partial(_outproj_kernel, alpha=alpha),
        out_shape=jax.ShapeDtypeStruct((m, d), F32),
        grid=(m // tm,),
        in_specs=[rows(oa.shape[1]), rows(ob.shape[1]), rows(gates.shape[1]), rows(d),
                  _resident(wa.shape), _resident(wb.shape), _resident(wo.shape),
                  _resident((1, d)), _resident((1, d))],
        out_specs=rows(d),
        compiler_params=_params("parallel"),
        name="out_proj",
    )(oa, ob, gates, h, wa, wb, wo, g, b)


def _rope_tables(pos, dim):
    half = dim // 2
    freqs = ROPE_THETA ** (-jnp.arange(half, dtype=F32) / half)
    ang = pos.astype(F32)[:, None] * freqs[None, :]
    cos, sin = jnp.cos(ang), jnp.sin(ang)
    reps = LANES // dim
    return (jnp.tile(jnp.concatenate([cos, cos], axis=1), (1, reps)),
            jnp.tile(jnp.concatenate([-sin, sin], axis=1), (1, reps)))


def _split_w_in(w_in, sizes):
    offs = [0]
    for sz in sizes:
        offs.append(offs[-1] + sz)
    return [w_in[:, offs[i]:offs[i + 1]] for i in range(len(sizes))]


def _pad_heads(w, n_heads, dim):
    d = w.shape[0]
    return jnp.pad(w.reshape(d, n_heads, dim), ((0, 0), (0, 0), (0, LANES - dim))).reshape(d, n_heads * LANES)


def _pad_cols(w, width):
    return jnp.pad(w, ((0, 0), (0, width - w.shape[1])))


def kernel(x_prompt, x_sample, cache_a_k, cache_a_v, cache_b_k, cache_b_v, cache_idx_k, page_table, ln_g, ln_b,
           ffn1_w_gate_up, ffn1_w_down, w_in, b_gate, diff_lambda_vecs, diff_subln_g, w_branch_a, w_branch_b,
           w_o, ffn2_w_gate_up, ffn2_w_down):
    depth = ln_g.shape[0]
    bsz, seq, d_model = x_prompt.shape
    dec_b, dec_t, _ = x_sample.shape
    n_pool, page, ha, _, da = cache_a_k.shape[1:]
    va_dim = cache_a_v.shape[4]
    hb, db = cache_b_k.shape[3:]
    idx_dim = cache_idx_k.shape[3]
    n_pages = page_table.shape[1]
    past = n_pages * page
    d_ff = ffn1_w_down.shape[1]
    w_qa, w_va, w_b = ha * 2 * da, ha * va_dim, hb * db
    n_idx_heads = (w_in.shape[2] - 2 * w_qa - w_va - 3 * w_b - idx_dim - 2 * d_model) // (idx_dim + 1)
    sizes = (w_qa, w_qa, w_va, w_b, w_b, w_b, n_idx_heads * idx_dim, idx_dim, n_idx_heads, 2 * d_model)
    assert sum(sizes) == w_in.shape[2]
    assert 2 * da == LANES and va_dim == LANES and db == 64 and idx_dim == 64 and hb % 2 == 0
    assert w_qa == w_va and da ** -0.5 == db ** -0.5 == idx_dim ** -0.5
    alpha = (2 * depth) ** 0.25
    q_scale = (da ** -0.5 * LOG2E, idx_dim ** -0.5)
    iw_scale = n_idx_heads ** -0.5
    iw_rows = -(-n_idx_heads // 16) * 16
    widths = (w_qa, w_va, w_b, w_b, n_idx_heads * LANES, d_model)
    dims_t = (w_qa, w_b, n_idx_heads * idx_dim, idx_dim, iw_rows, d_model)
    topk_prompt = min(TOPK_MAX, seq // 4)
    topk_sample = min(TOPK_MAX, (past + dec_t) // 4)

    cos_p, sin_p = _rope_tables(jnp.arange(seq), da)
    cos_s, sin_s = _rope_tables(jnp.tile(past + jnp.arange(dec_t), dec_b), da)
    pool_akt = jnp.transpose(cache_a_k, (0, 1, 3, 4, 5, 2)).reshape(depth, n_pool, w_qa, page)
    pool_bkt = jnp.transpose(cache_b_k, (0, 1, 3, 4, 2)).reshape(depth, n_pool, w_b, page)
    pool_bvt = jnp.transpose(cache_b_v, (0, 1, 3, 4, 2)).reshape(depth, n_pool, w_b, page)
    pool_ikt = jnp.transpose(cache_idx_k, (0, 1, 3, 2))

    h_p = x_prompt.reshape(bsz * seq, d_model)
    h_s = x_sample.reshape(dec_b * dec_t, d_model)
    states_p, states_s = [], []
    for layer in range(depth):
        lam_init = _lambda_init(layer)
        row = lambda v: v.reshape(1, -1)
        ffn1 = (ffn1_w_gate_up[layer][:, :d_ff].astype(BF16), ffn1_w_gate_up[layer][:, d_ff:].astype(BF16),
                ffn1_w_down[layer].astype(BF16), row(ln_g[layer, 0]), row(ln_b[layer, 0]))
        ffn2 = (ffn2_w_gate_up[layer][:, :d_ff].astype(BF16), ffn2_w_gate_up[layer][:, d_ff:].astype(BF16),
                ffn2_w_down[layer].astype(BF16), row(ln_g[layer, 2]), row(ln_b[layer, 2]))
        wqa, wka, wva, wqb, wkb, wvb, wiq, wik, wiw, wgl = _split_w_in(w_in[layer], sizes)
        w_all = jnp.concatenate([wqa, wka, wva, wqb, wkb, wvb, _pad_heads(wiq, n_idx_heads, idx_dim),
                                 _pad_cols(wik, LANES), _pad_cols(wiw, LANES), wgl], axis=1).astype(BF16)
        w_rm = jnp.concatenate([wka, wva, wkb, _pad_cols(wik, LANES), wgl], axis=1).astype(BF16)
        w_tr = jnp.concatenate([wqa, wka, wva, wqb, wkb, wvb, wiq, wik, _pad_cols(wiw, iw_rows)],
                               axis=1).T.astype(BF16)
        bg = row(b_gate[layer])
        lv = diff_lambda_vecs[layer]
        sg = row(diff_subln_g[layer])
        outw = (w_branch_a[layer].astype(BF16), w_branch_b[layer].astype(BF16), w_o[layer].astype(BF16),
                row(ln_g[layer, 1]), row(ln_b[layer, 1]))

        h1 = _ffn_ln(h_p, *ffn1, alpha)
        (ka, va4, kb, ik, gates, qat, kat, vat, qbt, kbt, vbt, vbtb, iqt, ikt, iwt) = _in_proj_t(
            h1.reshape(bsz, seq, d_model), cos_p, sin_p, cos_p.T, sin_p.T, w_rm, w_tr, bg, dims_t, ha,
            q_scale, iw_scale)
        oa = _diff_attn_prompt(qat, ka, vat, lv, sg, lam_init)
        ob = _dsa_prompt(qbt, kb, vbtb, iqt, ik, iwt, topk_prompt, n_idx_heads)
        h2 = _out_proj(oa.reshape(-1, w_va), ob.reshape(-1, w_b), gates.reshape(-1, 2 * d_model), h1, *outw, alpha)
        h_p = _ffn_ln(h2, *ffn2, alpha)
        states_p.append((jnp.transpose(kat.reshape(bsz, ha, 2, da, seq), (0, 4, 1, 2, 3)), va4,
                         jnp.transpose(kbt.reshape(bsz, hb, db, seq), (0, 3, 1, 2)),
                         jnp.transpose(vbt.reshape(bsz, hb, db, seq), (0, 3, 1, 2)),
                         jnp.transpose(ikt, (0, 2, 1))))

        h1 = _ffn_ln(h_s, *ffn1, alpha)
        outs = _in_proj(h1.reshape(1, dec_b * dec_t, d_model), cos_s, sin_s, w_all, bg, widths, idx_dim,
                        q_scale, iw_scale)
        qa, ka, va, qb, kb, vb, iq, ikp, ik, iw, gates = [o.reshape(dec_b, dec_t, -1) for o in outs]
        scores = _sample_scores(page_table, iq, iw, pool_ikt[layer], n_idx_heads)
        oa, ob = _sample_attn(page_table, lv, sg, qa, ka, va, qb, kb, vb, iq, iw, ikp, scores,
                              pool_akt[layer], cache_a_v[layer], pool_bkt[layer], pool_bvt[layer],
                              topk_sample, n_idx_heads, lam_init)
        h2 = _out_proj(oa.reshape(-1, w_va), ob.reshape(-1, w_b), gates.reshape(-1, 2 * d_model), h1, *outw, alpha)
        h_s = _ffn_ln(h2, *ffn2, alpha)
        states_s.append((ka.reshape(dec_b, dec_t, ha, 2, da), va.reshape(dec_b, dec_t, ha, va_dim),
                         kb.reshape(dec_b, dec_t, hb, db), vb.reshape(dec_b, dec_t, hb, db), ik))

    outs = [jnp.stack(s) for s in zip(*states_p)] + [jnp.stack(s) for s in zip(*states_s)]
    return (h_p.reshape(bsz, seq, d_model), h_s.reshape(dec_b, dec_t, d_model), *outs)
```

```python
import functools
import math

import numpy as np
import jax
import jax.numpy as jnp
from jax import lax
from jax.experimental import pallas as pl
from jax.experimental.pallas import tpu as pltpu

F32 = jnp.float32
BF16 = jnp.bfloat16
I32 = jnp.int32

LANES = 128
LN_EPS = 1e-5
ROPE_THETA = 10000.0
TOPK_MAX = 256
MASKED = -3.0e38
MASKED_TEST = -1.0e38
NEG_LOGIT = -1.0e30
LOG2E = math.log2(math.e)
INT_MAX = 2**31 - 1
INT_MIN = -2**31
VMEM_LIMIT = 56 * 1024 * 1024

TM_FFN = 512
TM_PROJ = 512
TM_PROJ_T = 256
F_CHUNK = 1408
T_DIFF = 1024
TQ_DSA = 256
TK_DSA = 512
QL = 128
KR = 256
COUNT_ROWS = 64
PAGES_ATTN = 8
PAGES_IDX = 16


def _lambda_init(layer):
    return 0.8 - 0.6 * math.exp(-0.3 * layer)


def _params(*sem):
    return pltpu.CompilerParams(dimension_semantics=sem, vmem_limit_bytes=VMEM_LIMIT)


def _resident(shape):
    nd = len(shape)
    return pl.BlockSpec(shape, lambda *_: (0,) * nd, pipeline_mode=pl.Buffered(1))


def _dot(a, b):
    return jnp.dot(a, b, preferred_element_type=F32)


def _dot_nt(a, b):
    return lax.dot_general(a, b, (((1,), (1,)), ((), ())), preferred_element_type=F32)


def _layer_norm(x, g, b):
    mu = jnp.mean(x, axis=-1, keepdims=True)
    xc = x - mu
    var = jnp.mean(xc * xc, axis=-1, keepdims=True)
    return xc * lax.rsqrt(var + LN_EPS) * g + b


def _half_mask(shape, c, axis):
    idx = lax.broadcasted_iota(I32, shape, axis)
    return (idx < 64) if c == 0 else (idx >= 64)


def _ffn_kernel(x_ref, wg_ref, wu_ref, wd_ref, g_ref, b_ref, o_ref, *, alpha, f_chunk):
    x = x_ref[...]
    xb = x.astype(BF16)
    d_ff = wg_ref.shape[1]
    acc = jnp.zeros(x.shape, F32)
    for c in range(0, d_ff, f_chunk):
        g = _dot(xb, wg_ref[:, c:c + f_chunk])
        u = _dot(xb, wu_ref[:, c:c + f_chunk])
        hmid = (g / (1.0 + jnp.exp(-g))) * u
        acc = acc + _dot(hmid.astype(BF16), wd_ref[c:c + f_chunk, :])
    o_ref[...] = _layer_norm(alpha * x + 0.5 * acc, g_ref[...], b_ref[...])


def _ffn_ln(x, wg, wu, wd, g, b, alpha):
    m, d = x.shape
    d_ff = wg.shape[1]
    tm = min(TM_FFN, m)
    f_chunk = F_CHUNK if d_ff % F_CHUNK == 0 else d_ff
    return pl.pallas_call(
        functools.partial(_ffn_kernel, alpha=alpha, f_chunk=f_chunk),
        out_shape=jax.ShapeDtypeStruct((m, d), F32),
        grid=(m // tm,),
        in_specs=[pl.BlockSpec((tm, d), lambda i: (i, 0)),
                  _resident((d, d_ff)), _resident((d, d_ff)), _resident((d_ff, d)),
                  _resident((1, d)), _resident((1, d))],
        out_specs=pl.BlockSpec((tm, d), lambda i: (i, 0)),
        compiler_params=_params("parallel"),
        name="ffn_ln",
    )(x, wg, wu, wd, g, b)


def _rope_rows(v, cos, sin):
    lane = lax.broadcasted_iota(I32, v.shape, 1)
    rot = jnp.where((lane % 64) < 32, pltpu.roll(v, LANES - 32, 1), pltpu.roll(v, 32, 1))
    return v * cos + rot * sin


def _rope_cols(v, cos_t, sin_t):
    parts = []
    for r in range(0, v.shape[0], 64):
        parts += [v[r + 32:r + 64], v[r:r + 32]]
    return v * cos_t + jnp.concatenate(parts, axis=0) * sin_t


def _sigmoid(v):
    return 1.0 / (1.0 + jnp.exp(-v))


def _inproj_kernel(h_ref, cos_ref, sin_ref, w_ref, bg_ref,
                   qa_ref, ka_ref, va_ref, qb_ref, kb_ref, vb_ref, iq_ref, ikp_ref, ik_ref, iw_ref, gates_ref,
                   *, widths, q_scale, iw_scale):
    xb = h_ref[0].astype(BF16)
    cos = cos_ref[...]
    sin = sin_ref[...]

    def project(off, width, out_ref, fn):
        v = _dot(xb, w_ref[:, off:off + width])
        for c in range(0, width, LANES):
            out_ref[0, :, c:c + LANES] = fn(v[:, c:c + LANES], c)

    attn_scale, idx_scale = q_scale
    rope_q = lambda v, c: _rope_rows(v, cos, sin) * attn_scale
    rope_iq = lambda v, c: _rope_rows(v, cos, sin) * idx_scale
    rope_k = lambda v, c: _rope_rows(v, cos, sin)
    ident = lambda v, c: v
    w_qa, w_va, w_qb, w_vb, w_iq, d_model = widths
    off = 0
    project(off, w_qa, qa_ref, rope_q); off += w_qa
    project(off, w_qa, ka_ref, rope_k); off += w_qa
    project(off, w_va, va_ref, ident); off += w_va
    project(off, w_qb, qb_ref, rope_q); off += w_qb
    project(off, w_qb, kb_ref, rope_k); off += w_qb
    project(off, w_vb, vb_ref, ident); off += w_vb
    project(off, w_iq, iq_ref, rope_iq); off += w_iq
    ikp = _rope_rows(_dot(xb, w_ref[:, off:off + LANES]), cos, sin); off += LANES
    ikp_ref[0] = ikp
    ik_ref[0] = ikp[:, :ik_ref.shape[2]]
    iw_ref[0] = _dot(xb, w_ref[:, off:off + LANES]) * iw_scale; off += LANES
    project(off, 2 * d_model, gates_ref, lambda v, c: _sigmoid(v + bg_ref[:, c:c + LANES]))


def _in_proj(h, cos_t, sin_t, w_all, b_gate, widths, idx_dim, q_scale, iw_scale):
    bsz, s, d = h.shape
    w_qa, w_va, w_qb, w_vb, w_iq, d_model = widths
    tm = min(TM_PROJ, s)
    out_w = (w_qa, w_qa, w_va, w_qb, w_qb, w_vb, w_iq, LANES, idx_dim, LANES, 2 * d_model)
    return pl.pallas_call(
        functools.partial(_inproj_kernel, widths=widths, q_scale=q_scale, iw_scale=iw_scale),
        out_shape=[jax.ShapeDtypeStruct((bsz, s, w), F32) for w in out_w],
        grid=(bsz, s // tm),
        in_specs=[pl.BlockSpec((1, tm, d), lambda b, i: (b, i, 0)),
                  pl.BlockSpec((tm, LANES), lambda b, i: (i, 0)),
                  pl.BlockSpec((tm, LANES), lambda b, i: (i, 0)),
                  _resident(w_all.shape), _resident(b_gate.shape)],
        out_specs=[pl.BlockSpec((1, tm, w), lambda b, i: (b, i, 0)) for w in out_w],
        compiler_params=_params("parallel", "parallel"),
        name="in_proj",
    )(h, cos_t, sin_t, w_all, b_gate)


def _inproj_t_kernel(h_ref, cos_ref, sin_ref, cost_ref, sint_ref, w_ref, wt_ref, bg_ref,
                     ka_ref, va4_ref, kb_ref, ik_ref, gates_ref,
                     qat_ref, kat_ref, vat_ref, qbt_ref, kbt_ref, vbt_ref, vbtb_ref, iqt_ref, ikt_ref, iwt_ref,
                     *, dims, q_scale, iw_scale):
    xb = h_ref[0].astype(BF16)
    cos = cos_ref[...]
    sin = sin_ref[...]
    cos_t = cost_ref[...]
    sin_t = sint_ref[...]
    w_a, w_b, w_iq, idx_dim, iw_rows, d_model = dims

    off = 0
    v = _dot(xb, w_ref[:, off:off + w_a]); off += w_a
    for c in range(0, w_a, LANES):
        ka_ref[0, :, c:c + LANES] = _rope_rows(v[:, c:c + LANES], cos, sin).astype(BF16)
    v = _dot(xb, w_ref[:, off:off + w_a]); off += w_a
    for hh in range(w_a // LANES):
        va4_ref[0, :, hh, :] = v[:, hh * LANES:(hh + 1) * LANES]
    v = _dot(xb, w_ref[:, off:off + w_b]); off += w_b
    for c in range(0, w_b, LANES):
        kb_ref[0, :, c:c + LANES] = _rope_rows(v[:, c:c + LANES], cos, sin).astype(BF16)
    ik_ref[0] = _rope_rows(_dot(xb, w_ref[:, off:off + LANES]), cos, sin)[:, :idx_dim].astype(BF16); off += LANES
    v = _dot(xb, w_ref[:, off:off + 2 * d_model])
    for c in range(0, 2 * d_model, LANES):
        gates_ref[0, :, c:c + LANES] = _sigmoid(v[:, c:c + LANES] + bg_ref[:, c:c + LANES])

    def project_t(off, width, fn):
        vt = _dot_nt(wt_ref[off:off + width, :], xb)
        for r in range(0, width, LANES):
            fn(r, vt[r:r + LANES])

    attn_scale, idx_scale = q_scale

    def rope_q_to(ref, scale):
        def fn(r, v):
            ref[0, r:r + LANES, :] = (_rope_cols(v, cos_t, sin_t) * scale).astype(BF16)
        return fn

    def rope_k_to(ref):
        def fn(r, v):
            ref[0, r:r + LANES, :] = _rope_cols(v, cos_t, sin_t)
        return fn

    def vat_fn(r, v):
        vat_ref[0, r:r + LANES, :] = v.astype(BF16)

    def vbt_fn(r, v):
        vbt_ref[0, r:r + LANES, :] = v
        vbtb_ref[0, r:r + LANES, :] = v.astype(BF16)

    off = 0
    project_t(off, w_a, rope_q_to(qat_ref, attn_scale)); off += w_a
    project_t(off, w_a, rope_k_to(kat_ref)); off += w_a
    project_t(off, w_a, vat_fn); off += w_a
    project_t(off, w_b, rope_q_to(qbt_ref, attn_scale)); off += w_b
    project_t(off, w_b, rope_k_to(kbt_ref)); off += w_b
    project_t(off, w_b, vbt_fn); off += w_b
    project_t(off, w_iq, rope_q_to(iqt_ref, idx_scale)); off += w_iq
    ikt_ref[0] = _rope_cols(_dot_nt(wt_ref[off:off + idx_dim, :], xb), cos_t[:idx_dim], sin_t[:idx_dim])
    off += idx_dim
    iwt_ref[0] = _dot_nt(wt_ref[off:off + iw_rows, :], xb) * iw_scale


def _in_proj_t(h, cos_r, sin_r, cos_c, sin_c, w_rm, w_tr, b_gate, dims, n_heads_a, q_scale, iw_scale):
    bsz, s, d = h.shape
    w_a, w_b, w_iq, idx_dim, iw_rows, d_model = dims
    tm = min(TM_PROJ_T, s)
    rows = lambda w, dt: (jax.ShapeDtypeStruct((bsz, s, w), dt), pl.BlockSpec((1, tm, w), lambda b, i: (b, i, 0)))
    cols = lambda w, dt: (jax.ShapeDtypeStruct((bsz, w, s), dt), pl.BlockSpec((1, w, tm), lambda b, i: (b, 0, i)))
    va4 = (jax.ShapeDtypeStruct((bsz, s, n_heads_a, LANES), F32),
           pl.BlockSpec((1, tm, n_heads_a, LANES), lambda b, i: (b, i, 0, 0)))
    outs = [rows(w_a, BF16), va4, rows(w_b, BF16), rows(idx_dim, BF16), rows(2 * d_model, F32),
            cols(w_a, BF16), cols(w_a, F32), cols(w_a, BF16), cols(w_b, BF16), cols(w_b, F32),
            cols(w_b, F32), cols(w_b, BF16), cols(w_iq, BF16), cols(idx_dim, F32), cols(iw_rows, F32)]
    return pl.pallas_call(
        functools.partial(_inproj_t_kernel, dims=dims, q_scale=q_scale, iw_scale=iw_scale),
        out_shape=[o[0] for o in outs],
        grid=(bsz, s // tm),
        in_specs=[pl.BlockSpec((1, tm, d), lambda b, i: (b, i, 0)),
                  pl.BlockSpec((tm, LANES), lambda b, i: (i, 0)),
                  pl.BlockSpec((tm, LANES), lambda b, i: (i, 0)),
                  pl.BlockSpec((LANES, tm), lambda b, i: (0, i)),
                  pl.BlockSpec((LANES, tm), lambda b, i: (0, i)),
                  _resident(w_rm.shape), _resident(w_tr.shape), _resident(b_gate.shape)],
        out_specs=[o[1] for o in outs],
        compiler_params=_params("parallel", "parallel"),
        name="in_proj_t",
    )(h, cos_r, sin_r, cos_c, sin_c, w_rm, w_tr, b_gate)


def _lam(lv_ref, lam_init):
    lv = lv_ref[...]
    s01 = jnp.sum(lv[0:1] * lv[1:2], axis=1, keepdims=True)
    s23 = jnp.sum(lv[2:3] * lv[3:4], axis=1, keepdims=True)
    return jnp.exp(s01) - jnp.exp(s23) + lam_init


def _head_norm(o, g, lam_init):
    return o * lax.rsqrt(jnp.mean(o * o, axis=-1, keepdims=True) + LN_EPS) * g * (1.0 - lam_init)


def _order_key_to_f32(u):
    bits = jnp.where(u < 0, u ^ INT_MIN, ~u)
    return lax.bitcast_convert_type(bits, F32)


def _kth_largest(count_ge, shape, topk, n_keys):
    def body(it, carry):
        prefix, cnt_at = carry
        cand = prefix | jnp.left_shift(jnp.int32(1), 31 - it)
        cnt = count_ge(_order_key_to_f32(cand))
        ok = cnt >= topk
        return jnp.where(ok, cand, prefix), jnp.where(ok, cnt, cnt_at)

    prefix, cnt_at = lax.fori_loop(0, 32, body, (jnp.zeros(shape, I32), jnp.full(shape, n_keys, F32)))
    return _order_key_to_f32(prefix), cnt_at


def _tie_bound(count_eq_below, need, n_bits, shape):
    def body(it, bound):
        cand = bound | jnp.left_shift(jnp.int32(1), n_bits - 1 - it)
        return jnp.where(count_eq_below(cand) <= need - 1.0, cand, bound)

    return lax.fori_loop(0, n_bits, body, jnp.zeros(shape, I32))


def _select_bias(sc, key, thr, bound):
    sel = ((sc > thr) | ((sc == thr) & (key <= bound))) & (sc > MASKED_TEST)
    return jnp.where(sel, 0.0, NEG_LOGIT)


def _softmax_tile(s, m, l, acc, pv, axis=1):
    m_new = jnp.maximum(m, jnp.max(s, axis=axis, keepdims=True))
    p = jnp.exp2(s - m_new)
    alpha = jnp.exp2(m - m_new)
    return m_new, alpha * l + jnp.sum(p, axis=axis, keepdims=True), alpha * acc + pv(p.astype(BF16))


def _causal_steps(n_q, last_of):
    return [(qi, ki) for qi in range(n_q) for ki in range(last_of(qi) + 1)]


def _mask_features(q, c):
    return jnp.where(_half_mask(q.shape, c, 0), q, jnp.zeros_like(q))


def _diffattn_kernel(qi_tab, ki_tab, lv_ref, g_ref, qt_ref, k_ref, vt_ref, o_ref, qz_ref, m_ref, l_ref, acc_ref,
                     *, t, lam_init):
    step = pl.program_id(2)
    qi = qi_tab[step]
    ki = ki_tab[step]

    @pl.when(ki == 0)
    def _():
        m_ref[...] = jnp.full(m_ref.shape, NEG_LOGIT, F32)
        l_ref[...] = jnp.zeros(l_ref.shape, F32)
        acc_ref[...] = jnp.zeros(acc_ref.shape, F32)
        for c in range(2):
            for q0 in range(0, t, QL):
                qz_ref[c, q0 // QL] = _mask_features(qt_ref[0, :, q0:q0 + QL], c)

    def sweep(diag):
        for q0 in range(0, t, QL):
            qs = slice(q0, q0 + QL)
            si = q0 // QL
            outs = []
            for c in range(2):
                m, l, acc = m_ref[c, si], l_ref[c, si], acc_ref[c, si]
                qz = qz_ref[c, si]
                for k0 in range(0, t, KR):
                    if diag and k0 > q0 + QL - 1:
                        continue
                    ks = slice(k0, k0 + KR)
                    s = _dot(k_ref[0, ks, :], qz)
                    if diag and k0 + KR - 1 > q0:
                        key = k0 + lax.broadcasted_iota(I32, (KR, QL), 0)
                        qry = q0 + lax.broadcasted_iota(I32, (KR, QL), 1)
                        s = s + jnp.where(key <= qry, 0.0, NEG_LOGIT)
                    m, l, acc = _softmax_tile(s, m, l, acc, lambda p: _dot(vt_ref[0, :, ks], p), axis=0)
                if diag:
                    outs.append(acc / l)
                else:
                    m_ref[c, si], l_ref[c, si], acc_ref[c, si] = m, l, acc
            if diag:
                o = outs[0] - _lam(lv_ref, lam_init) * outs[1]
                o = o * lax.rsqrt(jnp.mean(o * o, axis=0, keepdims=True) + LN_EPS) * g_ref[...] * (1.0 - lam_init)
                o_ref[0, qs, :] = o.T

    @pl.when(ki < qi)
    def _():
        sweep(False)

    @pl.when(ki == qi)
    def _():
        sweep(True)


def _diff_attn_prompt(qat, ka, vat, lv, subln_g, lam_init):
    bsz, width, s = qat.shape
    n_heads = width // LANES
    t = min(T_DIFF, s)
    steps = _causal_steps(s // t, lambda qi: qi)
    qi_tab = jnp.asarray(np.array([p[0] for p in steps], np.int32))
    ki_tab = jnp.asarray(np.array([p[1] for p in steps], np.int32))
    g_rows = jnp.broadcast_to(subln_g.reshape(LANES, 1), (LANES, QL))
    return pl.pallas_call(
        functools.partial(_diffattn_kernel, t=t, lam_init=lam_init),
        out_shape=jax.ShapeDtypeStruct((bsz, s, width), F32),
        grid_spec=pltpu.PrefetchScalarGridSpec(
            num_scalar_prefetch=2,
            grid=(bsz, n_heads, len(steps)),
            in_specs=[pl.BlockSpec(lv.shape, lambda b, h, i, qt, kt: (0, 0)),
                      pl.BlockSpec(g_rows.shape, lambda b, h, i, qt, kt: (0, 0)),
                      pl.BlockSpec((1, LANES, t), lambda b, h, i, qt, kt: (b, h, qt[i])),
                      pl.BlockSpec((1, t, LANES), lambda b, h, i, qt, kt: (b, kt[i], h)),
                      pl.BlockSpec((1, LANES, t), lambda b, h, i, qt, kt: (b, h, kt[i]))],
            out_specs=pl.BlockSpec((1, t, LANES), lambda b, h, i, qt, kt: (b, qt[i], h)),
            scratch_shapes=[pltpu.VMEM((2, t // QL, LANES, QL), BF16),
                            pltpu.VMEM((2, t // QL, 1, QL), F32), pltpu.VMEM((2, t // QL, 1, QL), F32),
                            pltpu.VMEM((2, t // QL, LANES, QL), F32)]),
        compiler_params=_params("parallel", "parallel", "arbitrary"),
        name="diff_attn_prompt",
    )(qi_tab, ki_tab, lv, g_rows, qat, ka, vat)


def _dsa_kernel(qi_tab, ki_tab, ph_tab, ikb_tab, kvb_tab,
                iqt_ref, iwt_ref, ik_ref, qt_ref, k_ref, vt_ref, o_ref,
                sc_ref, bias_ref, thr_ref, bound_ref, nge_ref, qz_ref, m_ref, l_ref, acc_ref,
                *, tq, tk, topk, n_idx_heads, idx_dim, n_key_bits):
    step = pl.program_id(1)
    qi = qi_tab[step]
    ki = ki_tab[step]
    phase = ph_tab[step]
    last = ((qi + 1) * tq - 1) // tk
    n_need = last + 1
    n_heads = 2 * (qt_ref.shape[1] // LANES)
    n_strips = tq // QL

    @pl.when((phase == 0) & (ki == 0))
    def _():
        m_ref[...] = jnp.full(m_ref.shape, NEG_LOGIT, F32)
        l_ref[...] = jnp.zeros(l_ref.shape, F32)
        acc_ref[...] = jnp.zeros(acc_ref.shape, F32)
        for pair in range(n_heads // 2):
            for si in range(n_strips):
                q = qt_ref[0, pair * LANES:(pair + 1) * LANES, si * QL:(si + 1) * QL]
                for c in range(2):
                    qz_ref[2 * pair + c, si] = _mask_features(q, c)

    @pl.when(phase == 0)
    def _():
        for si in range(n_strips):
            qs = slice(si * QL, (si + 1) * QL)
            for k0 in range(0, tk, LANES):
                ks = slice(k0, k0 + LANES)
                acc = jnp.zeros((LANES, QL), F32)
                for h in range(n_idx_heads):
                    s = _dot(ik_ref[0, ks, :], iqt_ref[0, h * idx_dim:(h + 1) * idx_dim, qs])
                    acc = acc + jnp.maximum(s, 0.0) * iwt_ref[0, h:h + 1, qs]
                key = ki * tk + k0 + lax.broadcasted_iota(I32, (LANES, QL), 0)
                qry = qi * tq + si * QL + lax.broadcasted_iota(I32, (LANES, QL), 1)
                sc_ref[ki, si, ks, :] = jnp.where(key <= qry, acc, MASKED)

    def count(si, pred):
        def body(c, acc):
            for r in range(0, tk, COUNT_ROWS):
                acc = acc + jnp.where(pred(sc_ref[c, si, r:r + COUNT_ROWS, :], c * tk + r), 1.0, 0.0)
            return acc
        acc = lax.fori_loop(0, n_need, body, jnp.zeros((COUNT_ROWS, QL), F32))
        return jnp.sum(acc, axis=0, keepdims=True)

    @pl.when((phase == 0) & (ki == last))
    def _():
        for si in range(n_strips):
            thr, n_ge = _kth_largest(lambda v: count(si, lambda sc, k0: sc >= v), (1, QL), topk, n_need * tk)
            thr_ref[si] = thr
            nge_ref[si] = n_ge
        bound_ref[...] = jnp.full(bound_ref.shape, INT_MAX, I32)

        @pl.when(jnp.max(nge_ref[...]) > topk)
        def _():
            row = lax.broadcasted_iota(I32, (COUNT_ROWS, QL), 0)
            for si in range(n_strips):
                thr = thr_ref[si]
                need = topk - count(si, lambda sc, k0: sc > thr)
                bound_ref[si] = _tie_bound(
                    lambda j: count(si, lambda sc, k0: (sc == thr) & (k0 + row < j)), need, n_key_bits, (1, QL))

    @pl.when(phase == 1)
    def _():
        key = ki * tk + lax.broadcasted_iota(I32, (tk, QL), 0)
        for si in range(n_strips):
            qs = slice(si * QL, (si + 1) * QL)
            bias_ref[si] = _select_bias(sc_ref[ki, si], key, thr_ref[si], bound_ref[si])
            for h in range(n_heads):
                feat = slice((h // 2) * LANES, (h // 2 + 1) * LANES)
                m, l, acc = m_ref[h, si], l_ref[h, si], acc_ref[h, si]
                qz = qz_ref[h, si]
                for k0 in range(0, tk, KR):
                    ks = slice(k0, k0 + KR)
                    s = _dot(k_ref[0, ks, feat], qz) + bias_ref[si, ks, :]
                    m, l, acc = _softmax_tile(s, m, l, acc, lambda p: _dot(vt_ref[0, feat, ks], p), axis=0)
                m_ref[h, si], l_ref[h, si], acc_ref[h, si] = m, l, acc

    @pl.when((phase == 1) & (ki == last))
    def _():
        for pair in range(n_heads // 2):
            for si in range(n_strips):
                lo = acc_ref[2 * pair, si] / l_ref[2 * pair, si]
                hi = acc_ref[2 * pair + 1, si] / l_ref[2 * pair + 1, si]
                o = jnp.where(_half_mask(lo.shape, 0, 0), lo, hi)
                o_ref[0, si * QL:(si + 1) * QL, pair * LANES:(pair + 1) * LANES] = o.T


def _dsa_prompt(qbt, kb, vbt, iqt, ik, iwt, topk, n_idx_heads):
    bsz, width, s = qbt.shape
    idx_dim = ik.shape[2]
    tq, tk = min(TQ_DSA, s), min(TK_DSA, s)
    n_heads = 2 * (width // LANES)
    n_strips = tq // QL
    last_of = lambda qi: ((qi + 1) * tq - 1) // tk
    pairs = _causal_steps(s // tq, last_of)
    tabs = {k: [] for k in ("qi", "ki", "ph", "ikb", "kvb")}
    for qi in range(s // tq):
        mine = [p[1] for p in pairs if p[0] == qi]
        for ph in range(2):
            for ki in mine:
                tabs["qi"].append(qi); tabs["ki"].append(ki); tabs["ph"].append(ph)
                tabs["ikb"].append(ki if ph == 0 else mine[-1])
                tabs["kvb"].append(ki if ph == 1 else 0)
    tab = [jnp.asarray(np.array(tabs[k], np.int32)) for k in ("qi", "ki", "ph", "ikb", "kvb")]

    return pl.pallas_call(
        functools.partial(_dsa_kernel, tq=tq, tk=tk, topk=topk, n_idx_heads=n_idx_heads, idx_dim=idx_dim,
                          n_key_bits=max(1, (s - 1).bit_length())),
        out_shape=jax.ShapeDtypeStruct((bsz, s, width), F32),
        grid_spec=pltpu.PrefetchScalarGridSpec(
            num_scalar_prefetch=5,
            grid=(bsz, len(tabs["qi"])),
            in_specs=[pl.BlockSpec((1, iqt.shape[1], tq), lambda b, i, qt, kt, pt, it, vt: (b, 0, qt[i])),
                      pl.BlockSpec((1, iwt.shape[1], tq), lambda b, i, qt, kt, pt, it, vt: (b, 0, qt[i])),
                      pl.BlockSpec((1, tk, idx_dim), lambda b, i, qt, kt, pt, it, vt: (b, it[i], 0)),
                      pl.BlockSpec((1, width, tq), lambda b, i, qt, kt, pt, it, vt: (b, 0, qt[i])),
                      pl.BlockSpec((1, tk, width), lambda b, i, qt, kt, pt, it, vt: (b, vt[i], 0)),
                      pl.BlockSpec((1, width, tk), lambda b, i, qt, kt, pt, it, vt: (b, 0, vt[i]))],
            out_specs=pl.BlockSpec((1, tq, width), lambda b, i, qt, kt, pt, it, vt: (b, qt[i], 0)),
            scratch_shapes=[pltpu.VMEM((s // tk, n_strips, tk, QL), F32), pltpu.VMEM((n_strips, tk, QL), F32),
                            pltpu.VMEM((n_strips, 1, QL), F32), pltpu.VMEM((n_strips, 1, QL), I32),
                            pltpu.VMEM((n_strips, 1, QL), F32),
                            pltpu.VMEM((n_heads, n_strips, LANES, QL), BF16),
                            pltpu.VMEM((n_heads, n_strips, 1, QL), F32),
                            pltpu.VMEM((n_heads, n_strips, 1, QL), F32),
                            pltpu.VMEM((n_heads, n_strips, LANES, QL), F32)]),
        compiler_params=_params("parallel", "arbitrary"),
        name="dsa_prompt",
    )(*tab, iqt, iwt, ik, qbt, kb, vbt)


def _stack_idx_queries(iq, iw, n_idx_heads, width):
    q = jnp.concatenate([iq[:, h * LANES:h * LANES + width] for h in range(n_idx_heads)], axis=0)
    w = jnp.concatenate([iw[:, h:h + 1] for h in range(n_idx_heads)], axis=0)
    return q.astype(BF16), w


def _sum_heads(x, n_idx_heads, t):
    out = x[0:t]
    for h in range(1, n_idx_heads):
        out = out + x[h * t:(h + 1) * t]
    return out


def _sample_scores_kernel(pt_ref, iq_ref, iw_ref, *refs, n_idx_heads):
    ik_refs, o_ref = refs[:-1], refs[-1]
    t = iq_ref.shape[1]
    page = ik_refs[0].shape[2]
    q, w = _stack_idx_queries(iq_ref[0], iw_ref[0], n_idx_heads, ik_refs[0].shape[1])
    for g, ik_ref in enumerate(ik_refs):
        s = jnp.maximum(_dot(q, ik_ref[0].astype(BF16)), 0.0) * w
        o_ref[0, :, g * page:(g + 1) * page] = _sum_heads(s, n_idx_heads, t)


def _sample_scores(page_table, iq, iw, pool_ikt, n_idx_heads):
    bsz, t, _ = iq.shape
    n_pages = page_table.shape[1]
    idx_dim, page = pool_ikt.shape[1], pool_ikt.shape[2]
    grp = math.gcd(PAGES_IDX, n_pages)

    def paged(g):
        return pl.BlockSpec((1, idx_dim, page), lambda b, p, pt: (pt[b, p * grp + g], 0, 0))

    return pl.pallas_call(
        functools.partial(_sample_scores_kernel, n_idx_heads=n_idx_heads),
        out_shape=jax.ShapeDtypeStruct((bsz, t, n_pages * page), F32),
        grid_spec=pltpu.PrefetchScalarGridSpec(
            num_scalar_prefetch=1,
            grid=(bsz, n_pages // grp),
            in_specs=[pl.BlockSpec((1, t, iq.shape[2]), lambda b, p, pt: (b, 0, 0)),
                      pl.BlockSpec((1, t, LANES), lambda b, p, pt: (b, 0, 0))] + [paged(g) for g in range(grp)],
            out_specs=pl.BlockSpec((1, t, grp * page), lambda b, p, pt: (b, 0, p))),
        compiler_params=_params("parallel", "arbitrary"),
        name="sample_idx_scores",
    )(page_table, iq, iw, *([pool_ikt] * grp))


def _sample_attn_kernel(pt_ref, lv_ref, g_ref, qa_ref, kan_ref, van_ref, qb_ref, kbn_ref, vbn_ref,
                        iq_ref, iw_ref, ikn_ref, scall_ref, scgrp_ref, *refs,
                        grp, topk, n_idx_heads, n_col_bits, lam_init):
    pak_refs, pav_refs, pbk_refs, pbv_refs = (refs[i * grp:(i + 1) * grp] for i in range(4))
    oa_ref, ob_ref, thr_ref, bound_ref, scn_ref, qz_ref, m_ref, l_ref, acc_ref = refs[4 * grp:]
    p_id = pl.program_id(1)
    n_steps = pl.num_programs(1)
    t = qa_ref.shape[1]
    page = pak_refs[0].shape[2]
    past = scall_ref.shape[2]
    n_a = qa_ref.shape[2] // LANES
    n_bp = qb_ref.shape[2] // LANES
    n_units = n_a + n_bp
    rowi = lax.broadcasted_iota(I32, (t, page), 0)
    coli = lax.broadcasted_iota(I32, (t, page), 1)
    new_visible = (coli <= rowi) & (coli < t)

    def pad_rows(x):
        return jnp.concatenate([x, jnp.zeros((page - x.shape[0], x.shape[1]), x.dtype)], axis=0)

    def stacked(q):
        return jnp.concatenate([jnp.where(_half_mask(q.shape, c, 1), q, 0.0) for c in range(2)],
                               axis=0).astype(BF16)

    @pl.when(p_id == 0)
    def _():
        m_ref[...] = jnp.full(m_ref.shape, NEG_LOGIT, F32)
        l_ref[...] = jnp.zeros(l_ref.shape, F32)
        acc_ref[...] = jnp.zeros(acc_ref.shape, F32)
        for h in range(n_a):
            qz_ref[h] = stacked(qa_ref[0, :, h * LANES:(h + 1) * LANES])
        for pair in range(n_bp):
            qz_ref[n_a + pair] = stacked(qb_ref[0, :, pair * LANES:(pair + 1) * LANES])
        q, w = _stack_idx_queries(iq_ref[0], iw_ref[0], n_idx_heads, LANES)
        scn = _sum_heads(jnp.maximum(_dot_nt(q, pad_rows(ikn_ref[0]).astype(BF16)), 0.0) * w, n_idx_heads, t)
        scn = jnp.where(new_visible, scn, MASKED)
        scn_ref[...] = scn
        sc_all = scall_ref[0]
        col_all = lax.broadcasted_iota(I32, (t, past), 1)

        def count(pred_past, pred_new):
            return (jnp.sum(jnp.where(pred_past(sc_all), 1.0, 0.0), axis=1, keepdims=True)
                    + jnp.sum(jnp.where(pred_new(scn), 1.0, 0.0), axis=1, keepdims=True))

        thr, n_ge = _kth_largest(lambda v: count(lambda s: s >= v, lambda s: s >= v), (t, 1), topk, past + page)
        thr_ref[...] = thr
        bound_ref[...] = jnp.full((t, 1), INT_MAX, I32)

        @pl.when(jnp.max(n_ge) > topk)
        def _():
            need = topk - count(lambda s: s > thr, lambda s: s > thr)
            bound_ref[...] = _tie_bound(
                lambda j: count(lambda s: (s == thr) & (col_all < j), lambda s: (s == thr) & (past + coli < j)),
                need, n_col_bits, (t, 1))

    def attend(logits, bias_b, pv_fns):
        s = jnp.concatenate(logits[:n_a] + [x + bias_b for x in logits[n_a:]], axis=0)
        m_new = jnp.maximum(m_ref[...], jnp.max(s, axis=1, keepdims=True))
        p = jnp.exp2(s - m_new).astype(BF16)
        alpha = jnp.exp2(m_ref[...] - m_new)
        l_new = alpha * l_ref[...] + jnp.sum(p.astype(F32), axis=1, keepdims=True)
        pv = jnp.concatenate([fn(p[u * 2 * t:(u + 1) * 2 * t]) for u, fn in enumerate(pv_fns)], axis=0)
        return m_new, l_new, alpha * acc_ref[...] + pv

    twice = lambda b: jnp.concatenate([b, b], axis=0)
    thr = thr_ref[...]
    bound = bound_ref[...]
    width = grp * page
    colg = p_id * width + lax.broadcasted_iota(I32, (t, width), 1)
    bias_b = twice(_select_bias(scgrp_ref[0], colg, thr, bound))
    logits, pv_fns = [], []
    for h in range(n_a):
        feat = slice(h * LANES, (h + 1) * LANES)
        kt = jnp.concatenate([r[0, feat, :] for r in pak_refs], axis=1).astype(BF16)
        logits.append(_dot(qz_ref[h], kt))
        pv_fns.append(lambda p, h=h: _dot(
            p, jnp.concatenate([r[0, :, h, :] for r in pav_refs], axis=0).astype(BF16)))
    for pair in range(n_bp):
        feat = slice(pair * LANES, (pair + 1) * LANES)
        kt = jnp.concatenate([r[0, feat, :] for r in pbk_refs], axis=1).astype(BF16)
        logits.append(_dot(qz_ref[n_a + pair], kt))
        pv_fns.append(lambda p, feat=feat: _dot_nt(
            p, jnp.concatenate([r[0, feat, :] for r in pbv_refs], axis=1).astype(BF16)))
    m_new, l_new, acc_new = attend(logits, bias_b, pv_fns)
    m_ref[...] = m_new
    l_ref[...] = l_new
    acc_ref[...] = acc_new

    @pl.when(p_id == n_steps - 1)
    def _():
        bias_a = twice(jnp.where(new_visible, 0.0, NEG_LOGIT))
        bias_n = twice(_select_bias(scn_ref[...], past + coli, thr, bound))
        logits, pv_fns = [], []
        for h in range(n_a):
            feat = slice(h * LANES, (h + 1) * LANES)
            logits.append(_dot_nt(qz_ref[h], pad_rows(kan_ref[0, :, feat]).astype(BF16)) + bias_a)
            pv_fns.append(lambda p, feat=feat: _dot(p, pad_rows(van_ref[0, :, feat]).astype(BF16)))
        for pair in range(n_bp):
            feat = slice(pair * LANES, (pair + 1) * LANES)
            logits.append(_dot_nt(qz_ref[n_a + pair], pad_rows(kbn_ref[0, :, feat]).astype(BF16)))
            pv_fns.append(lambda p, feat=feat: _dot(p, pad_rows(vbn_ref[0, :, feat]).astype(BF16)))
        _, l_fin, acc_fin = attend(logits, bias_n, pv_fns)
        o = acc_fin / l_fin
        lam = _lam(lv_ref, lam_init)
        for h in range(n_a):
            r = 2 * t * h
            oa_ref[0, :, h * LANES:(h + 1) * LANES] = _head_norm(
                o[r:r + t] - lam * o[r + t:r + 2 * t], g_ref[...], lam_init)
        for pair in range(n_bp):
            r = 2 * t * (n_a + pair)
            ob_ref[0, :, pair * LANES:(pair + 1) * LANES] = jnp.where(
                _half_mask((t, LANES), 0, 1), o[r:r + t], o[r + t:r + 2 * t])


def _sample_attn(page_table, lv, subln_g, qa, ka, va, qb, kb, vb, iq, iw, ikp, scores,
                 pool_akt, pool_av, pool_bkt, pool_bvt, topk, n_idx_heads, lam_init):
    bsz, t, wa = qa.shape
    wb = qb.shape[2]
    n_pages = page_table.shape[1]
    page = pool_akt.shape[2]
    past = n_pages * page
    grp = math.gcd(PAGES_ATTN, n_pages)
    n_rows = 2 * t * (wa // LANES + wb // LANES)

    def per_batch(shape):
        return pl.BlockSpec((1,) + tuple(shape[1:]), lambda b, p, pt: (b, 0, 0))

    def paged(shape, g):
        nd = len(shape) - 1
        return pl.BlockSpec((1,) + tuple(shape[1:]), lambda b, p, pt: (pt[b, p * grp + g],) + (0,) * nd)

    pools = [pool_akt, pool_av, pool_bkt, pool_bvt]
    return pl.pallas_call(
        functools.partial(_sample_attn_kernel, grp=grp, topk=topk, n_idx_heads=n_idx_heads,
                          n_col_bits=max(1, (past + page - 1).bit_length()), lam_init=lam_init),
        out_shape=[jax.ShapeDtypeStruct((bsz, t, wa), F32), jax.ShapeDtypeStruct((bsz, t, wb), F32)],
        grid_spec=pltpu.PrefetchScalarGridSpec(
            num_scalar_prefetch=1,
            grid=(bsz, n_pages // grp),
            in_specs=[pl.BlockSpec(lv.shape, lambda b, p, pt: (0, 0)),
                      pl.BlockSpec(subln_g.shape, lambda b, p, pt: (0, 0)),
                      per_batch(qa.shape), per_batch(ka.shape), per_batch(va.shape),
                      per_batch(qb.shape), per_batch(kb.shape), per_batch(vb.shape),
                      per_batch(iq.shape), per_batch(iw.shape), per_batch(ikp.shape),
                      per_batch(scores.shape),
                      pl.BlockSpec((1, t, grp * page), lambda b, p, pt: (b, 0, p))]
                     + [paged(pool.shape, g) for pool in pools for g in range(grp)],
            out_specs=[per_batch((bsz, t, wa)), per_batch((bsz, t, wb))],
            scratch_shapes=[pltpu.VMEM((t, 1), F32), pltpu.VMEM((t, 1), I32), pltpu.VMEM((t, page), F32),
                            pltpu.VMEM((n_rows // (2 * t), 2 * t, LANES), BF16),
                            pltpu.VMEM((n_rows, 1), F32), pltpu.VMEM((n_rows, 1), F32),
                            pltpu.VMEM((n_rows, LANES), F32)]),
        compiler_params=_params("parallel", "arbitrary"),
        name="sample_attn",
    )(page_table, lv, subln_g, qa, ka, va, qb, kb, vb, iq, iw, ikp, scores, scores,
      *[pool for pool in pools for _ in range(grp)])


def _outproj_kernel(oa_ref, ob_ref, gates_ref, h_ref, wa_ref, wb_ref, wo_ref, g_ref, b_ref, o_ref, *, alpha):
    ya = _dot(oa_ref[...].astype(BF16), wa_ref[...])
    yb = _dot(ob_ref[...].astype(BF16), wb_ref[...])
    d = ya.shape[1]
    mixed = gates_ref[:, :d] * ya + gates_ref[:, d:] * yb
    o_ref[...] = _layer_norm(alpha * h_ref[...] + _dot(mixed.astype(BF16), wo_ref[...]), g_ref[...], b_ref[...])


def _out_proj(oa, ob, gates, h, wa, wb, wo, g, b, alpha):
    m, d = h.shape
    tm = min(TM_PROJ, m)

    def rows(width):
        return pl.BlockSpec((tm, width), lambda i: (i, 0))

    return pl.pallas_call(
        functools.partial(_outproj_kernel, alpha=alpha),
        out_shape=jax.ShapeDtypeStruct((m, d), F32),
        grid=(m // tm,),
        in_specs=[rows(oa.shape[1]), rows(ob.shape[1]), rows(gates.shape[1]), rows(d),
                  _resident(wa.shape), _resident(wb.shape), _resident(wo.shape),
                  _resident((1, d)), _resident((1, d))],
        out_specs=rows(d),
        compiler_params=_params("parallel"),
        name="out_proj",
    )(oa, ob, gates, h, wa, wb, wo, g, b)


def _rope_tables(pos, dim):
    half = dim // 2
    freqs = ROPE_THETA ** (-jnp.arange(half, dtype=F32) / half)
    ang = pos.astype(F32)[:, None] * freqs[None, :]
    cos, sin = jnp.cos(ang), jnp.sin(ang)
    reps = LANES // dim
    return (jnp.tile(jnp.concatenate([cos, cos], axis=1), (1, reps)),
            jnp.tile(jnp.concatenate([-sin, sin], axis=1), (1, reps)))


def _split_w_in(w_in, sizes):
    offs = [0]
    for sz in sizes:
        offs.append(offs[-1] + sz)
    return [w_in[:, offs[i]:offs[i + 1]] for i in range(len(sizes))]


def _pad_heads(w, n_heads, dim):
    d = w.shape[0]
    return jnp.pad(w.reshape(d, n_heads, dim), ((0, 0), (0, 0), (0, LANES - dim))).reshape(d, n_heads * LANES)


def _pad_cols(w, width):
    return jnp.pad(w, ((0, 0), (0, width - w.shape[1])))


def kernel(x_prompt, x_sample, cache_a_k, cache_a_v, cache_b_k, cache_b_v, cache_idx_k, page_table, ln_g, ln_b,
           ffn1_w_gate_up, ffn1_w_down, w_in, b_gate, diff_lambda_vecs, diff_subln_g, w_branch_a, w_branch_b,
           w_o, ffn2_w_gate_up, ffn2_w_down):
    depth = ln_g.shape[0]
    bsz, seq, d_model = x_prompt.shape
    dec_b, dec_t, _ = x_sample.shape
    n_pool, page, ha, _, da = cache_a_k.shape[1:]
    va_dim = cache_a_v.shape[4]
    hb, db = cache_b_k.shape[3:]
    idx_dim = cache_idx_k.shape[3]
    n_pages = page_table.shape[1]
    past = n_pages * page
    d_ff = ffn1_w_down.shape[1]
    w_qa, w_va, w_b = ha * 2 * da, ha * va_dim, hb * db
    n_idx_heads = (w_in.shape[2] - 2 * w_qa - w_va - 3 * w_b - idx_dim - 2 * d_model) // (idx_dim + 1)
    sizes = (w_qa, w_qa, w_va, w_b, w_b, w_b, n_idx_heads * idx_dim, idx_dim, n_idx_heads, 2 * d_model)
    assert sum(sizes) == w_in.shape[2]
    assert 2 * da == LANES and va_dim == LANES and db == 64 and idx_dim == 64 and hb % 2 == 0
    assert w_qa == w_va and da ** -0.5 == db ** -0.5 == idx_dim ** -0.5
    alpha = (2 * depth) ** 0.25
    q_scale = (da ** -0.5 * LOG2E, idx_dim ** -0.5)
    iw_scale = n_idx_heads ** -0.5
    iw_rows = -(-n_idx_heads // 16) * 16
    widths = (w_qa, w_va, w_b, w_b, n_idx_heads * LANES, d_model)
    dims_t = (w_qa, w_b, n_idx_heads * idx_dim, idx_dim, iw_rows, d_model)
    topk_prompt = min(TOPK_MAX, seq // 4)
    topk_sample = min(TOPK_MAX, (past + dec_t) // 4)

    cos_p, sin_p = _rope_tables(jnp.arange(seq), da)
    cos_s, sin_s = _rope_tables(jnp.tile(past + jnp.arange(dec_t), dec_b), da)
    pool_akt = jnp.transpose(cache_a_k, (0, 1, 3, 4, 5, 2)).reshape(depth, n_pool, w_qa, page)
    pool_bkt = jnp.transpose(cache_b_k, (0, 1, 3, 4, 2)).reshape(depth, n_pool, w_b, page)
    pool_bvt = jnp.transpose(cache_b_v, (0, 1, 3, 4, 2)).reshape(depth, n_pool, w_b, page)
    pool_ikt = jnp.transpose(cache_idx_k, (0, 1, 3, 2))

    h_p = x_prompt.reshape(bsz * seq, d_model)
    h_s = x_sample.reshape(dec_b * dec_t, d_model)
    states_p, states_s = [], []
    for layer in range(depth):
        lam_init = _lambda_init(layer)
        row = lambda v: v.reshape(1, -1)
        ffn1 = (ffn1_w_gate_up[layer][:, :d_ff].astype(BF16), ffn1_w_gate_up[layer][:, d_ff:].astype(BF16),
                ffn1_w_down[layer].astype(BF16), row(ln_g[layer, 0]), row(ln_b[layer, 0]))
        ffn2 = (ffn2_w_gate_up[layer][:, :d_ff].astype(BF16), ffn2_w_gate_up[layer][:, d_ff:].astype(BF16),
                ffn2_w_down[layer].astype(BF16), row(ln_g[layer, 2]), row(ln_b[layer, 2]))
        wqa, wka, wva, wqb, wkb, wvb, wiq, wik, wiw, wgl = _split_w_in(w_in[layer], sizes)
        w_all = jnp.concatenate([wqa, wka, wva, wqb, wkb, wvb, _pad_heads(wiq, n_idx_heads, idx_dim),
                                 _pad_cols(wik, LANES), _pad_cols(wiw, LANES), wgl], axis=1).astype(BF16)
        w_rm = jnp.concatenate([wka, wva, wkb, _pad_cols(wik, LANES), wgl], axis=1).astype(BF16)
        w_tr = jnp.concatenate([wqa, wka, wva, wqb, wkb, wvb, wiq, wik, _pad_cols(wiw, iw_rows)],
                               axis=1).T.astype(BF16)
        bg = row(b_gate[layer])
        lv = diff_lambda_vecs[layer]
        sg = row(diff_subln_g[layer])
        outw = (w_branch_a[layer].astype(BF16), w_branch_b[layer].astype(BF16), w_o[layer].astype(BF16),
                row(ln_g[layer, 1]), row(ln_b[layer, 1]))

        h1 = _ffn_ln(h_p, *ffn1, alpha)
        (ka, va4, kb, ik, gates, qat, kat, vat, qbt, kbt, vbt, vbtb, iqt, ikt, iwt) = _in_proj_t(
            h1.reshape(bsz, seq, d_model), cos_p, sin_p, cos_p.T, sin_p.T, w_rm, w_tr, bg, dims_t, ha,
            q_scale, iw_scale)
        oa = _diff_attn_prompt(qat, ka, vat, lv, sg, lam_init)
        ob = _dsa_prompt(qbt, kb, vbtb, iqt, ik, iwt, topk_prompt, n_idx_heads)
        h2 = _out_proj(oa.reshape(-1, w_va), ob.reshape(-1, w_b), gates.reshape(-1, 2 * d_model), h1, *outw, alpha)
        h_p = _ffn_ln(h2, *ffn2, alpha)
        states_p.append((jnp.transpose(kat.reshape(bsz, ha, 2, da, seq), (0, 4, 1, 2, 3)), va4,
                         jnp.transpose(kbt.reshape(bsz, hb, db, seq), (0, 3, 1, 2)),
                         jnp.transpose(vbt.reshape(bsz, hb, db, seq), (0, 3, 1, 2)),
                         jnp.transpose(ikt, (0, 2, 1))))

        h1 = _ffn_ln(h_s, *ffn1, alpha)
        outs = _in_proj(h1.reshape(1, dec_b * dec_t, d_model), cos_s, sin_s, w_all, bg, widths, idx_dim,
                        q_scale, iw_scale)
        qa, ka, va, qb, kb, vb, iq, ikp, ik, iw, gates = [o.reshape(dec_b, dec_t, -1) for o in outs]
        scores = _sample_scores(page_table, iq, iw, pool_ikt[layer], n_idx_heads)
        oa, ob = _sample_attn(page_table, lv, sg, qa, ka, va, qb, kb, vb, iq, iw, ikp, scores,
                              pool_akt[layer], cache_a_v[layer], pool_bkt[layer], pool_bvt[layer],
                              topk_sample, n_idx_heads, lam_init)
        h2 = _out_proj(oa.reshape(-1, w_va), ob.reshape(-1, w_b), gates.reshape(-1, 2 * d_model), h1, *outw, alpha)
        h_s = _ffn_ln(h2, *ffn2, alpha)
        states_s.append((ka.reshape(dec_b, dec_t, ha, 2, da), va.reshape(dec_b, dec_t, ha, va_dim),
                         kb.reshape(dec_b, dec_t, hb, db), vb.reshape(dec_b, dec_t, hb, db), ik))

    outs = [jnp.stack(s) for s in zip(*states_p)] + [jnp.stack(s) for s in zip(*states_s)]
    return (h_p.reshape(bsz, seq, d_model), h_s.reshape(dec_b, dec_t, d_model), *outs)
```

```python
import functools
import math

import numpy as np
import jax
import jax.numpy as jnp
from jax import lax
from jax.experimental import pallas as pl
from jax.experimental.pallas import tpu as pltpu

F32 = jnp.float32
BF16 = jnp.bfloat16
I32 = jnp.int32

LANES = 128
LN_EPS = 1e-5
ROPE_THETA = 10000.0
TOPK_MAX = 256
MASKED = -3.0e38
MASKED_TEST = -1.0e38
NEG_LOGIT = -1.0e30
LOG2E = math.log2(math.e)
INT_MAX = 2**31 - 1
INT_MIN = -2**31
VMEM_LIMIT = 56 * 1024 * 1024

TM_FFN = 512
TM_PROJ = 512
TM_PROJ_T = 256
F_CHUNK = 1408
T_DIFF = 1024
TQ_DSA = 256
TK_DSA = 512
QL = 128
KR = 256
COUNT_ROWS = 64
PAGES_ATTN = 8
PAGES_IDX = 16


def _lambda_init(layer):
    return 0.8 - 0.6 * math.exp(-0.3 * layer)


def _params(*sem):
    return pltpu.CompilerParams(dimension_semantics=sem, vmem_limit_bytes=VMEM_LIMIT)


def _resident(shape):
    nd = len(shape)
    return pl.BlockSpec(shape, lambda *_: (0,) * nd, pipeline_mode=pl.Buffered(1))


def _dot(a, b):
    return jnp.dot(a, b, preferred_element_type=F32)


def _dot_nt(a, b):
    return lax.dot_general(a, b, (((1,), (1,)), ((), ())), preferred_element_type=F32)


def _layer_norm(x, g, b):
    mu = jnp.mean(x, axis=-1, keepdims=True)
    xc = x - mu
    var = jnp.mean(xc * xc, axis=-1, keepdims=True)
    return xc * lax.rsqrt(var + LN_EPS) * g + b


def _half_mask(shape, c, axis):
    idx = lax.broadcasted_iota(I32, shape, axis)
    return (idx < 64) if c == 0 else (idx >= 64)


def _ffn_kernel(x_ref, wg_ref, wu_ref, wd_ref, g_ref, b_ref, o_ref, *, alpha, f_chunk):
    x = x_ref[...]
    xb = x.astype(BF16)
    d_ff = wg_ref.shape[1]
    acc = jnp.zeros(x.shape, F32)
    for c in range(0, d_ff, f_chunk):
        g = _dot(xb, wg_ref[:, c:c + f_chunk])
        u = _dot(xb, wu_ref[:, c:c + f_chunk])
        hmid = (g / (1.0 + jnp.exp(-g))) * u
        acc = acc + _dot(hmid.astype(BF16), wd_ref[c:c + f_chunk, :])
    o_ref[...] = _layer_norm(alpha * x + 0.5 * acc, g_ref[...], b_ref[...])


def _ffn_ln(x, wg, wu, wd, g, b, alpha):
    m, d = x.shape
    d_ff = wg.shape[1]
    tm = min(TM_FFN, m)
    f_chunk = F_CHUNK if d_ff % F_CHUNK == 0 else d_ff
    return pl.pallas_call(
        functools.partial(_ffn_kernel, alpha=alpha, f_chunk=f_chunk),
        out_shape=jax.ShapeDtypeStruct((m, d), F32),
        grid=(m // tm,),
        in_specs=[pl.BlockSpec((tm, d), lambda i: (i, 0)),
                  _resident((d, d_ff)), _resident((d, d_ff)), _resident((d_ff, d)),
                  _resident((1, d)), _resident((1, d))],
        out_specs=pl.BlockSpec((tm, d), lambda i: (i, 0)),
        compiler_params=_params("parallel"),
        name="ffn_ln",
    )(x, wg, wu, wd, g, b)


def _rope_rows(v, cos, sin):
    lane = lax.broadcasted_iota(I32, v.shape, 1)
    rot = jnp.where((lane % 64) < 32, pltpu.roll(v, LANES - 32, 1), pltpu.roll(v, 32, 1))
    return v * cos + rot * sin


def _rope_cols(v, cos_t, sin_t):
    parts = []
    for r in range(0, v.shape[0], 64):
        parts += [v[r + 32:r + 64], v[r:r + 32]]
    return v * cos_t + jnp.concatenate(parts, axis=0) * sin_t


def _sigmoid(v):
    return 1.0 / (1.0 + jnp.exp(-v))


def _inproj_kernel(h_ref, cos_ref, sin_ref, w_ref, bg_ref,
                   qa_ref, ka_ref, va_ref, qb_ref, kb_ref, vb_ref, iq_ref, ikp_ref, ik_ref, iw_ref, gates_ref,
                   *, widths, q_scale, iw_scale):
    xb = h_ref[0].astype(BF16)
    cos = cos_ref[...]
    sin = sin_ref[...]

    def project(off, width, out_ref, fn):
        v = _dot(xb, w_ref[:, off:off + width])
        for c in range(0, width, LANES):
            out_ref[0, :, c:c + LANES] = fn(v[:, c:c + LANES], c)

    attn_scale, idx_scale = q_scale
    rope_q = lambda v, c: _rope_rows(v, cos, sin) * attn_scale
    rope_iq = lambda v, c: _rope_rows(v, cos, sin) * idx_scale
    rope_k = lambda v, c: _rope_rows(v, cos, sin)
    ident = lambda v, c: v
    w_qa, w_va, w_qb, w_vb, w_iq, d_model = widths
    off = 0
    project(off, w_qa, qa_ref, rope_q); off += w_qa
    project(off, w_qa, ka_ref, rope_k); off += w_qa
    project(off, w_va, va_ref, ident); off += w_va
    project(off, w_qb, qb_ref, rope_q); off += w_qb
    project(off, w_qb, kb_ref, rope_k); off += w_qb
    project(off, w_vb, vb_ref, ident); off += w_vb
    project(off, w_iq, iq_ref, rope_iq); off += w_iq
    ikp = _rope_rows(_dot(xb, w_ref[:, off:off + LANES]), cos, sin); off += LANES
    ikp_ref[0] = ikp
    ik_ref[0] = ikp[:, :ik_ref.shape[2]]
    iw_ref[0] = _dot(xb, w_ref[:, off:off + LANES]) * iw_scale; off += LANES
    project(off, 2 * d_model, gates_ref, lambda v, c: _sigmoid(v + bg_ref[:, c:c + LANES]))


def _in_proj(h, cos_t, sin_t, w_all, b_gate, widths, idx_dim, q_scale, iw_scale):
    bsz, s, d = h.shape
    w_qa, w_va, w_qb, w_vb, w_iq, d_model = widths
    tm = min(TM_PROJ, s)
    out_w = (w_qa, w_qa, w_va, w_qb, w_qb, w_vb, w_iq, LANES, idx_dim, LANES, 2 * d_model)
    return pl.pallas_call(
        functools.partial(_inproj_kernel, widths=widths, q_scale=q_scale, iw_scale=iw_scale),
        out_shape=[jax.ShapeDtypeStruct((bsz, s, w), F32) for w in out_w],
        grid=(bsz, s // tm),
        in_specs=[pl.BlockSpec((1, tm, d), lambda b, i: (b, i, 0)),
                  pl.BlockSpec((tm, LANES), lambda b, i: (i, 0)),
                  pl.BlockSpec((tm, LANES), lambda b, i: (i, 0)),
                  _resident(w_all.shape), _resident(b_gate.shape)],
        out_specs=[pl.BlockSpec((1, tm, w), lambda b, i: (b, i, 0)) for w in out_w],
        compiler_params=_params("parallel", "parallel"),
        name="in_proj",
    )(h, cos_t, sin_t, w_all, b_gate)


def _inproj_t_kernel(h_ref, cos_ref, sin_ref, cost_ref, sint_ref, w_ref, wt_ref, bg_ref,
                     ka_ref, va4_ref, kb_ref, ik_ref, gates_ref,
                     qat_ref, kat_ref, vat_ref, qbt_ref, kbt_ref, vbt_ref, vbtb_ref, iqt_ref, ikt_ref, iwt_ref,
                     *, dims, q_scale, iw_scale):
    xb = h_ref[0].astype(BF16)
    cos = cos_ref[...]
    sin = sin_ref[...]
    cos_t = cost_ref[...]
    sin_t = sint_ref[...]
    w_a, w_b, w_iq, idx_dim, iw_rows, d_model = dims

    off = 0
    v = _dot(xb, w_ref[:, off:off + w_a]); off += w_a
    for c in range(0, w_a, LANES):
        ka_ref[0, :, c:c + LANES] = _rope_rows(v[:, c:c + LANES], cos, sin).astype(BF16)
    v = _dot(xb, w_ref[:, off:off + w_a]); off += w_a
    for hh in range(w_a // LANES):
        va4_ref[0, :, hh, :] = v[:, hh * LANES:(hh + 1) * LANES]
    v = _dot(xb, w_ref[:, off:off + w_b]); off += w_b
    for c in range(0, w_b, LANES):
        kb_ref[0, :, c:c + LANES] = _rope_rows(v[:, c:c + LANES], cos, sin).astype(BF16)
    ik_ref[0] = _rope_rows(_dot(xb, w_ref[:, off:off + LANES]), cos, sin)[:, :idx_dim].astype(BF16); off += LANES
    v = _dot(xb, w_ref[:, off:off + 2 * d_model])
    for c in range(0, 2 * d_model, LANES):
        gates_ref[0, :, c:c + LANES] = _sigmoid(v[:, c:c + LANES] + bg_ref[:, c:c + LANES])

    def project_t(off, width, fn):
        vt = _dot_nt(wt_ref[off:off + width, :], xb)
        for r in range(0, width, LANES):
            fn(r, vt[r:r + LANES])

    attn_scale, idx_scale = q_scale

    def rope_q_to(ref, scale):
        def fn(r, v):
            ref[0, r:r + LANES, :] = (_rope_cols(v, cos_t, sin_t) * scale).astype(BF16)
        return fn

    def rope_k_to(ref):
        def fn(r, v):
            ref[0, r:r + LANES, :] = _rope_cols(v, cos_t, sin_t)
        return fn

    def vat_fn(r, v):
        vat_ref[0, r:r + LANES, :] = v.astype(BF16)

    def vbt_fn(r, v):
        vbt_ref[0, r:r + LANES, :] = v
        vbtb_ref[0, r:r + LANES, :] = v.astype(BF16)

    off = 0
    project_t(off, w_a, rope_q_to(qat_ref, attn_scale)); off += w_a
    project_t(off, w_a, rope_k_to(kat_ref)); off += w_a
    project_t(off, w_a, vat_fn); off += w_a
    project_t(off, w_b, rope_q_to(qbt_ref, attn_scale)); off += w_b
    project_t(off, w_b, rope_k_to(kbt_ref)); off += w_b
    project_t(off, w_b, vbt_fn); off += w_b
    project_t(off, w_iq, rope_q_to(iqt_ref, idx_scale)); off += w_iq
    ikt_ref[0] = _rope_cols(_dot_nt(wt_ref[off:off + idx_dim, :], xb), cos_t[:idx_dim], sin_t[:idx_dim])
    off += idx_dim
    iwt_ref[0] = _dot_nt(wt_ref[off:off + iw_rows, :], xb) * iw_scale


def _in_proj_t(h, cos_r, sin_r, cos_c, sin_c, w_rm, w_tr, b_gate, dims, n_heads_a, q_scale, iw_scale):
    bsz, s, d = h.shape
    w_a, w_b, w_iq, idx_dim, iw_rows, d_model = dims
    tm = min(TM_PROJ_T, s)
    rows = lambda w, dt: (jax.ShapeDtypeStruct((bsz, s, w), dt), pl.BlockSpec((1, tm, w), lambda b, i: (b, i, 0)))
    cols = lambda w, dt: (jax.ShapeDtypeStruct((bsz, w, s), dt), pl.BlockSpec((1, w, tm), lambda b, i: (b, 0, i)))
    va4 = (jax.ShapeDtypeStruct((bsz, s, n_heads_a, LANES), F32),
           pl.BlockSpec((1, tm, n_heads_a, LANES), lambda b, i: (b, i, 0, 0)))
    outs = [rows(w_a, BF16), va4, rows(w_b, BF16), rows(idx_dim, BF16), rows(2 * d_model, F32),
            cols(w_a, BF16), cols(w_a, F32), cols(w_a, BF16), cols(w_b, BF16), cols(w_b, F32),
            cols(w_b, F32), cols(w_b, BF16), cols(w_iq, BF16), cols(idx_dim, F32), cols(iw_rows, F32)]
    return pl.pallas_call(
        functools.partial(_inproj_t_kernel, dims=dims, q_scale=q_scale, iw_scale=iw_scale),
        out_shape=[o[0] for o in outs],
        grid=(bsz, s // tm),
        in_specs=[pl.BlockSpec((1, tm, d), lambda b, i: (b, i, 0)),
                  pl.BlockSpec((tm, LANES), lambda b, i: (i, 0)),
                  pl.BlockSpec((tm, LANES), lambda b, i: (i, 0)),
                  pl.BlockSpec((LANES, tm), lambda b, i: (0, i)),
                  pl.BlockSpec((LANES, tm), lambda b, i: (0, i)),
                  _resident(w_rm.shape), _resident(w_tr.shape), _resident(b_gate.shape)],
        out_specs=[o[1] for o in outs],
        compiler_params=_params("parallel", "parallel"),
        name="in_proj_t",
    )(h, cos_r, sin_r, cos_c, sin_c, w_rm, w_tr, b_gate)


def _lam(lv_ref, lam_init):
    lv = lv_ref[...]
    s01 = jnp.sum(lv[0:1] * lv[1:2], axis=1, keepdims=True)
    s23 = jnp.sum(lv[2:3] * lv[3:4], axis=1, keepdims=True)
    return jnp.exp(s01) - jnp.exp(s23) + lam_init


def _head_norm(o, g, lam_init):
    return o * lax.rsqrt(jnp.mean(o * o, axis=-1, keepdims=True) + LN_EPS) * g * (1.0 - lam_init)


def _order_key_to_f32(u):
    bits = jnp.where(u < 0, u ^ INT_MIN, ~u)
    return lax.bitcast_convert_type(bits, F32)


def _kth_largest(count_ge, shape, topk, n_keys):
    def body(it, carry):
        prefix, cnt_at = carry
        cand = prefix | jnp.left_shift(jnp.int32(1), 31 - it)
        cnt = count_ge(_order_key_to_f32(cand))
        ok = cnt >= topk
        return jnp.where(ok, cand, prefix), jnp.where(ok, cnt, cnt_at)

    prefix, cnt_at = lax.fori_loop(0, 32, body, (jnp.zeros(shape, I32), jnp.full(shape, n_keys, F32)))
    return _order_key_to_f32(prefix), cnt_at


def _tie_bound(count_eq_below, need, n_bits, shape):
    def body(it, bound):
        cand = bound | jnp.left_shift(jnp.int32(1), n_bits - 1 - it)
        return jnp.where(count_eq_below(cand) <= need - 1.0, cand, bound)

    return lax.fori_loop(0, n_bits, body, jnp.zeros(shape, I32))


def _select_bias(sc, key, thr, bound):
    sel = ((sc > thr) | ((sc == thr) & (key <= bound))) & (sc > MASKED_TEST)
    return jnp.where(sel, 0.0, NEG_LOGIT)


def _softmax_tile(s, m, l, acc, pv, axis=1):
    m_new = jnp.maximum(m, jnp.max(s, axis=axis, keepdims=True))
    p = jnp.exp2(s - m_new)
    alpha = jnp.exp2(m - m_new)
    return m_new, alpha * l + jnp.sum(p, axis=axis, keepdims=True), alpha * acc + pv(p.astype(BF16))


def _causal_steps(n_q, last_of):
    return [(qi, ki) for qi in range(n_q) for ki in range(last_of(qi) + 1)]


def _mask_features(q, c):
    return jnp.where(_half_mask(q.shape, c, 0), q, jnp.zeros_like(q))


def _diffattn_kernel(qi_tab, ki_tab, lv_ref, g_ref, qt_ref, k_ref, vt_ref, o_ref, qz_ref, m_ref, l_ref, acc_ref,
                     *, t, lam_init):
    step = pl.program_id(2)
    qi = qi_tab[step]
    ki = ki_tab[step]

    @pl.when(ki == 0)
    def _():
        m_ref[...] = jnp.full(m_ref.shape, NEG_LOGIT, F32)
        l_ref[...] = jnp.zeros(l_ref.shape, F32)
        acc_ref[...] = jnp.zeros(acc_ref.shape, F32)
        for c in range(2):
            for q0 in range(0, t, QL):
                qz_ref[c, q0 // QL] = _mask_features(qt_ref[0, :, q0:q0 + QL], c)

    def sweep(diag):
        for q0 in range(0, t, QL):
            qs = slice(q0, q0 + QL)
            si = q0 // QL
            outs = []
            for c in range(2):
                m, l, acc = m_ref[c, si], l_ref[c, si], acc_ref[c, si]
                qz = qz_ref[c, si]
                for k0 in range(0, t, KR):
                    if diag and k0 > q0 + QL - 1:
                        continue
                    ks = slice(k0, k0 + KR)
                    s = _dot(k_ref[0, ks, :], qz)
                    if diag and k0 + KR - 1 > q0:
                        key = k0 + lax.broadcasted_iota(I32, (KR, QL), 0)
                        qry = q0 + lax.broadcasted_iota(I32, (KR, QL), 1)
                        s = s + jnp.where(key <= qry, 0.0, NEG_LOGIT)
                    m, l, acc = _softmax_tile(s, m, l, acc, lambda p: _dot(vt_ref[0, :, ks], p), axis=0)
                if diag:
                    outs.append(acc / l)
                else:
                    m_ref[c, si], l_ref[c, si], acc_ref[c, si] = m, l, acc
            if diag:
                o = outs[0] - _lam(lv_ref, lam_init) * outs[1]
                o = o * lax.rsqrt(jnp.mean(o * o, axis=0, keepdims=True) + LN_EPS) * g_ref[...] * (1.0 - lam_init)
                o_ref[0, qs, :] = o.T

    @pl.when(ki < qi)
    def _():
        sweep(False)

    @pl.when(ki == qi)
    def _():
        sweep(True)


def _diff_attn_prompt(qat, ka, vat, lv, subln_g, lam_init):
    bsz, width, s = qat.shape
    n_heads = width // LANES
    t = min(T_DIFF, s)
    steps = _causal_steps(s // t, lambda qi: qi)
    qi_tab = jnp.asarray(np.array([p[0] for p in steps], np.int32))
    ki_tab = jnp.asarray(np.array([p[1] for p in steps], np.int32))
    g_rows = jnp.broadcast_to(subln_g.reshape(LANES, 1), (LANES, QL))
    return pl.pallas_call(
        functools.partial(_diffattn_kernel, t=t, lam_init=lam_init),
        out_shape=jax.ShapeDtypeStruct((bsz, s, width), F32),
        grid_spec=pltpu.PrefetchScalarGridSpec(
            num_scalar_prefetch=2,
            grid=(bsz, n_heads, len(steps)),
            in_specs=[pl.BlockSpec(lv.shape, lambda b, h, i, qt, kt: (0, 0)),
                      pl.BlockSpec(g_rows.shape, lambda b, h, i, qt, kt: (0, 0)),
                      pl.BlockSpec((1, LANES, t), lambda b, h, i, qt, kt: (b, h, qt[i])),
                      pl.BlockSpec((1, t, LANES), lambda b, h, i, qt, kt: (b, kt[i], h)),
                      pl.BlockSpec((1, LANES, t), lambda b, h, i, qt, kt: (b, h, kt[i]))],
            out_specs=pl.BlockSpec((1, t, LANES), lambda b, h, i, qt, kt: (b, qt[i], h)),
            scratch_shapes=[pltpu.VMEM((2, t // QL, LANES, QL), BF16),
                            pltpu.VMEM((2, t // QL, 1, QL), F32), pltpu.VMEM((2, t // QL, 1, QL), F32),
                            pltpu.VMEM((2, t // QL, LANES, QL), F32)]),
        compiler_params=_params("parallel", "parallel", "arbitrary"),
        name="diff_attn_prompt",
    )(qi_tab, ki_tab, lv, g_rows, qat, ka, vat)


def _dsa_kernel(qi_tab, ki_tab, ph_tab, kvb_tab,
                iqt_ref, iwt_ref, ik_ref, qt_ref, k_ref, vt_ref, o_ref,
                sc_ref, bias_ref, thr_ref, bound_ref, nge_ref, qz_ref, m_ref, l_ref, acc_ref,
                *, tq, tk, topk, n_idx_heads, idx_dim, n_key_bits):
    step = pl.program_id(1)
    qi = qi_tab[step]
    ki = ki_tab[step]
    phase = ph_tab[step]
    last = ((qi + 1) * tq - 1) // tk
    n_need = last + 1
    n_heads = 2 * (qt_ref.shape[1] // LANES)
    n_strips = tq // QL

    @pl.when(phase == 0)
    def _():
        m_ref[...] = jnp.full(m_ref.shape, NEG_LOGIT, F32)
        l_ref[...] = jnp.zeros(l_ref.shape, F32)
        acc_ref[...] = jnp.zeros(acc_ref.shape, F32)
        for pair in range(n_heads // 2):
            for si in range(n_strips):
                q = qt_ref[0, pair * LANES:(pair + 1) * LANES, si * QL:(si + 1) * QL]
                for c in range(2):
                    qz_ref[2 * pair + c, si] = _mask_features(q, c)

    @pl.when(phase == 0)
    def _():
        def score_block(c, carry):
            for si in range(n_strips):
                qs = slice(si * QL, (si + 1) * QL)
                for k0 in range(0, tk, LANES):
                    keys = pl.ds(pl.multiple_of(c * tk + k0, LANES), LANES)
                    acc = jnp.zeros((LANES, QL), F32)
                    for h in range(n_idx_heads):
                        s = _dot(ik_ref[0, keys, :], iqt_ref[0, h * idx_dim:(h + 1) * idx_dim, qs])
                        acc = acc + jnp.maximum(s, 0.0) * iwt_ref[0, h:h + 1, qs]
                    key = c * tk + k0 + lax.broadcasted_iota(I32, (LANES, QL), 0)
                    qry = qi * tq + si * QL + lax.broadcasted_iota(I32, (LANES, QL), 1)
                    sc_ref[c, si, k0:k0 + LANES, :] = jnp.where(key <= qry, acc, MASKED)
            return carry

        lax.fori_loop(0, n_need, score_block, 0)

    def count(si, pred):
        def body(c, acc):
            for r in range(0, tk, COUNT_ROWS):
                acc = acc + jnp.where(pred(sc_ref[c, si, r:r + COUNT_ROWS, :], c * tk + r), 1.0, 0.0)
            return acc
        acc = lax.fori_loop(0, n_need, body, jnp.zeros((COUNT_ROWS, QL), F32))
        return jnp.sum(acc, axis=0, keepdims=True)

    @pl.when(phase == 0)
    def _():
        for si in range(n_strips):
            thr, n_ge = _kth_largest(lambda v: count(si, lambda sc, k0: sc >= v), (1, QL), topk, n_need * tk)
            thr_ref[si] = thr
            nge_ref[si] = n_ge
        bound_ref[...] = jnp.full(bound_ref.shape, INT_MAX, I32)

        @pl.when(jnp.max(nge_ref[...]) > topk)
        def _():
            row = lax.broadcasted_iota(I32, (COUNT_ROWS, QL), 0)
            for si in range(n_strips):
                thr = thr_ref[si]
                need = topk - count(si, lambda sc, k0: sc > thr)
                bound_ref[si] = _tie_bound(
                    lambda j: count(si, lambda sc, k0: (sc == thr) & (k0 + row < j)), need, n_key_bits, (1, QL))

    @pl.when(phase == 1)
    def _():
        key = ki * tk + lax.broadcasted_iota(I32, (tk, QL), 0)
        for si in range(n_strips):
            qs = slice(si * QL, (si + 1) * QL)
            bias_ref[si] = _select_bias(sc_ref[ki, si], key, thr_ref[si], bound_ref[si])
            for h in range(n_heads):
                feat = slice((h // 2) * LANES, (h // 2 + 1) * LANES)
                m, l, acc = m_ref[h, si], l_ref[h, si], acc_ref[h, si]
                qz = qz_ref[h, si]
                for k0 in range(0, tk, KR):
                    ks = slice(k0, k0 + KR)
                    s = _dot(k_ref[0, ks, feat], qz) + bias_ref[si, ks, :]
                    m, l, acc = _softmax_tile(s, m, l, acc, lambda p: _dot(vt_ref[0, feat, ks], p), axis=0)
                m_ref[h, si], l_ref[h, si], acc_ref[h, si] = m, l, acc

    @pl.when((phase == 1) & (ki == last))
    def _():
        for pair in range(n_heads // 2):
            for si in range(n_strips):
                lo = acc_ref[2 * pair, si] / l_ref[2 * pair, si]
                hi = acc_ref[2 * pair + 1, si] / l_ref[2 * pair + 1, si]
                o = jnp.where(_half_mask(lo.shape, 0, 0), lo, hi)
                o_ref[0, si * QL:(si + 1) * QL, pair * LANES:(pair + 1) * LANES] = o.T


def _dsa_prompt(qbt, kb, vbt, iqt, ik, iwt, topk, n_idx_heads):
    bsz, width, s = qbt.shape
    idx_dim = ik.shape[2]
    tq, tk = min(TQ_DSA, s), min(TK_DSA, s)
    n_heads = 2 * (width // LANES)
    n_strips = tq // QL
    last_of = lambda qi: ((qi + 1) * tq - 1) // tk
    pairs = _causal_steps(s // tq, last_of)
    tabs = {k: [] for k in ("qi", "ki", "ph", "kvb")}
    for qi in range(s // tq):
        mine = [p[1] for p in pairs if p[0] == qi]
        for ph, kis in ((0, mine[-1:]), (1, mine)):
            for ki in kis:
                tabs["qi"].append(qi); tabs["ki"].append(ki); tabs["ph"].append(ph)
                tabs["kvb"].append(ki if ph == 1 else 0)
    tab = [jnp.asarray(np.array(tabs[k], np.int32)) for k in ("qi", "ki", "ph", "kvb")]

    return pl.pallas_call(
        functools.partial(_dsa_kernel, tq=tq, tk=tk, topk=topk, n_idx_heads=n_idx_heads, idx_dim=idx_dim,
                          n_key_bits=max(1, (s - 1).bit_length())),
        out_shape=jax.ShapeDtypeStruct((bsz, s, width), F32),
        grid_spec=pltpu.PrefetchScalarGridSpec(
            num_scalar_prefetch=4,
            grid=(bsz, len(tabs["qi"])),
            in_specs=[pl.BlockSpec((1, iqt.shape[1], tq), lambda b, i, qt, kt, pt, vt: (b, 0, qt[i])),
                      pl.BlockSpec((1, iwt.shape[1], tq), lambda b, i, qt, kt, pt, vt: (b, 0, qt[i])),
                      pl.BlockSpec((1, s, idx_dim), lambda b, i, qt, kt, pt, vt: (b, 0, 0)),
                      pl.BlockSpec((1, width, tq), lambda b, i, qt, kt, pt, vt: (b, 0, qt[i])),
                      pl.BlockSpec((1, tk, width), lambda b, i, qt, kt, pt, vt: (b, vt[i], 0)),
                      pl.BlockSpec((1, width, tk), lambda b, i, qt, kt, pt, vt: (b, 0, vt[i]))],
            out_specs=pl.BlockSpec((1, tq, width), lambda b, i, qt, kt, pt, vt: (b, qt[i], 0)),
            scratch_shapes=[pltpu.VMEM((s // tk, n_strips, tk, QL), F32), pltpu.VMEM((n_strips, tk, QL), F32),
                            pltpu.VMEM((n_strips, 1, QL), F32), pltpu.VMEM((n_strips, 1, QL), I32),
                            pltpu.VMEM((n_strips, 1, QL), F32),
                            pltpu.VMEM((n_heads, n_strips, LANES, QL), BF16),
                            pltpu.VMEM((n_heads, n_strips, 1, QL), F32),
                            pltpu.VMEM((n_heads, n_strips, 1, QL), F32),
                            pltpu.VMEM((n_heads, n_strips, LANES, QL), F32)]),
        compiler_params=_params("parallel", "arbitrary"),
        name="dsa_prompt",
    )(*tab, iqt, iwt, ik, qbt, kb, vbt)


def _stack_idx_queries(iq, iw, n_idx_heads, width):
    q = jnp.concatenate([iq[:, h * LANES:h * LANES + width] for h in range(n_idx_heads)], axis=0)
    w = jnp.concatenate([iw[:, h:h + 1] for h in range(n_idx_heads)], axis=0)
    return q.astype(BF16), w


def _sum_heads(x, n_idx_heads, t):
    out = x[0:t]
    for h in range(1, n_idx_heads):
        out = out + x[h * t:(h + 1) * t]
    return out


def _sample_scores_kernel(pt_ref, iq_ref, iw_ref, *refs, n_idx_heads):
    ik_refs, o_ref = refs[:-1], refs[-1]
    t = iq_ref.shape[1]
    page = ik_refs[0].shape[2]
    q, w = _stack_idx_queries(iq_ref[0], iw_ref[0], n_idx_heads, ik_refs[0].shape[1])
    for g, ik_ref in enumerate(ik_refs):
        s = jnp.maximum(_dot(q, ik_ref[0].astype(BF16)), 0.0) * w
        o_ref[0, :, g * page:(g + 1) * page] = _sum_heads(s, n_idx_heads, t)


def _sample_scores(page_table, iq, iw, pool_ikt, n_idx_heads):
    bsz, t, _ = iq.shape
    n_pages = page_table.shape[1]
    idx_dim, page = pool_ikt.shape[1], pool_ikt.shape[2]
    grp = math.gcd(PAGES_IDX, n_pages)

    def paged(g):
        return pl.BlockSpec((1, idx_dim, page), lambda b, p, pt: (pt[b, p * grp + g], 0, 0))

    return pl.pallas_call(
        functools.partial(_sample_scores_kernel, n_idx_heads=n_idx_heads),
        out_shape=jax.ShapeDtypeStruct((bsz, t, n_pages * page), F32),
        grid_spec=pltpu.PrefetchScalarGridSpec(
            num_scalar_prefetch=1,
            grid=(bsz, n_pages // grp),
            in_specs=[pl.BlockSpec((1, t, iq.shape[2]), lambda b, p, pt: (b, 0, 0)),
                      pl.BlockSpec((1, t, LANES), lambda b, p, pt: (b, 0, 0))] + [paged(g) for g in range(grp)],
            out_specs=pl.BlockSpec((1, t, grp * page), lambda b, p, pt: (b, 0, p))),
        compiler_params=_params("parallel", "arbitrary"),
        name="sample_idx_scores",
    )(page_table, iq, iw, *([pool_ikt] * grp))


def _sample_attn_kernel(pt_ref, lv_ref, g_ref, qa_ref, kan_ref, van_ref, qb_ref, kbn_ref, vbn_ref,
                        iq_ref, iw_ref, ikn_ref, scall_ref, scgrp_ref, *refs,
                        grp, topk, n_idx_heads, n_col_bits, lam_init):
    pak_refs, pav_refs, pbk_refs, pbv_refs = (refs[i * grp:(i + 1) * grp] for i in range(4))
    oa_ref, ob_ref, thr_ref, bound_ref, scn_ref, qz_ref, m_ref, l_ref, acc_ref = refs[4 * grp:]
    p_id = pl.program_id(1)
    n_steps = pl.num_programs(1)
    t = qa_ref.shape[1]
    page = pak_refs[0].shape[2]
    past = scall_ref.shape[2]
    n_a = qa_ref.shape[2] // LANES
    n_bp = qb_ref.shape[2] // LANES
    n_units = n_a + n_bp
    rowi = lax.broadcasted_iota(I32, (t, page), 0)
    coli = lax.broadcasted_iota(I32, (t, page), 1)
    new_visible = (coli <= rowi) & (coli < t)

    def pad_rows(x):
        return jnp.concatenate([x, jnp.zeros((page - x.shape[0], x.shape[1]), x.dtype)], axis=0)

    def stacked(q):
        return jnp.concatenate([jnp.where(_half_mask(q.shape, c, 1), q, 0.0) for c in range(2)],
                               axis=0).astype(BF16)

    @pl.when(p_id == 0)
    def _():
        m_ref[...] = jnp.full(m_ref.shape, NEG_LOGIT, F32)
        l_ref[...] = jnp.zeros(l_ref.shape, F32)
        acc_ref[...] = jnp.zeros(acc_ref.shape, F32)
        for h in range(n_a):
            qz_ref[h] = stacked(qa_ref[0, :, h * LANES:(h + 1) * LANES])
        for pair in range(n_bp):
            qz_ref[n_a + pair] = stacked(qb_ref[0, :, pair * LANES:(pair + 1) * LANES])
        q, w = _stack_idx_queries(iq_ref[0], iw_ref[0], n_idx_heads, LANES)
        scn = _sum_heads(jnp.maximum(_dot_nt(q, pad_rows(ikn_ref[0]).astype(BF16)), 0.0) * w, n_idx_heads, t)
        scn = jnp.where(new_visible, scn, MASKED)
        scn_ref[...] = scn
        sc_all = scall_ref[0]
        col_all = lax.broadcasted_iota(I32, (t, past), 1)

        def count(pred_past, pred_new):
            return (jnp.sum(jnp.where(pred_past(sc_all), 1.0, 0.0), axis=1, keepdims=True)
                    + jnp.sum(jnp.where(pred_new(scn), 1.0, 0.0), axis=1, keepdims=True))

        thr, n_ge = _kth_largest(lambda v: count(lambda s: s >= v, lambda s: s >= v), (t, 1), topk, past + page)
        thr_ref[...] = thr
        bound_ref[...] = jnp.full((t, 1), INT_MAX, I32)

        @pl.when(jnp.max(n_ge) > topk)
        def _():
            need = topk - count(lambda s: s > thr, lambda s: s > thr)
            bound_ref[...] = _tie_bound(
                lambda j: count(lambda s: (s == thr) & (col_all < j), lambda s: (s == thr) & (past + coli < j)),
                need, n_col_bits, (t, 1))

    def attend(logits, bias_b, pv_fns):
        s = jnp.concatenate(logits[:n_a] + [x + bias_b for x in logits[n_a:]], axis=0)
        m_new = jnp.maximum(m_ref[...], jnp.max(s, axis=1, keepdims=True))
        p = jnp.exp2(s - m_new).astype(BF16)
        alpha = jnp.exp2(m_ref[...] - m_new)
        l_new = alpha * l_ref[...] + jnp.sum(p.astype(F32), axis=1, keepdims=True)
        pv = jnp.concatenate([fn(p[u * 2 * t:(u + 1) * 2 * t]) for u, fn in enumerate(pv_fns)], axis=0)
        return m_new, l_new, alpha * acc_ref[...] + pv

    twice = lambda b: jnp.concatenate([b, b], axis=0)
    thr = thr_ref[...]
    bound = bound_ref[...]
    width = grp * page
    colg = p_id * width + lax.broadcasted_iota(I32, (t, width), 1)
    bias_b = twice(_select_bias(scgrp_ref[0], colg, thr, bound))
    logits, pv_fns = [], []
    for h in range(n_a):
        feat = slice(h * LANES, (h + 1) * LANES)
        kt = jnp.concatenate([r[0, feat, :] for r in pak_refs], axis=1).astype(BF16)
        logits.append(_dot(qz_ref[h], kt))
        pv_fns.append(lambda p, h=h: _dot(
            p, jnp.concatenate([r[0, :, h, :] for r in pav_refs], axis=0).astype(BF16)))
    for pair in range(n_bp):
        feat = slice(pair * LANES, (pair + 1) * LANES)
        kt = jnp.concatenate([r[0, feat, :] for r in pbk_refs], axis=1).astype(BF16)
        logits.append(_dot(qz_ref[n_a + pair], kt))
        pv_fns.append(lambda p, feat=feat: _dot_nt(
            p, jnp.concatenate([r[0, feat, :] for r in pbv_refs], axis=1).astype(BF16)))
    m_new, l_new, acc_new = attend(logits, bias_b, pv_fns)
    m_ref[...] = m_new
    l_ref[...] = l_new
    acc_ref[...] = acc_new

    @pl.when(p_id == n_steps - 1)
    def _():
        bias_a = twice(jnp.where(new_visible, 0.0, NEG_LOGIT))
        bias_n = twice(_select_bias(scn_ref[...], past + coli, thr, bound))
        logits, pv_fns = [], []
        for h in range(n_a):
            feat = slice(h * LANES, (h + 1) * LANES)
            logits.append(_dot_nt(qz_ref[h], pad_rows(kan_ref[0, :, feat]).astype(BF16)) + bias_a)
            pv_fns.append(lambda p, feat=feat: _dot(p, pad_rows(van_ref[0, :, feat]).astype(BF16)))
        for pair in range(n_bp):
            feat = slice(pair * LANES, (pair + 1) * LANES)
            logits.append(_dot_nt(qz_ref[n_a + pair], pad_rows(kbn_ref[0, :, feat]).astype(BF16)))
            pv_fns.append(lambda p, feat=feat: _dot(p, pad_rows(vbn_ref[0, :, feat]).astype(BF16)))
        _, l_fin, acc_fin = attend(logits, bias_n, pv_fns)
        o = acc_fin / l_fin
        lam = _lam(lv_ref, lam_init)
        for h in range(n_a):
            r = 2 * t * h
            oa_ref[0, :, h * LANES:(h + 1) * LANES] = _head_norm(
                o[r:r + t] - lam * o[r + t:r + 2 * t], g_ref[...], lam_init)
        for pair in range(n_bp):
            r = 2 * t * (n_a + pair)
            ob_ref[0, :, pair * LANES:(pair + 1) * LANES] = jnp.where(
                _half_mask((t, LANES), 0, 1), o[r:r + t], o[r + t:r + 2 * t])


def _sample_attn(page_table, lv, subln_g, qa, ka, va, qb, kb, vb, iq, iw, ikp, scores,
                 pool_akt, pool_av, pool_bkt, pool_bvt, topk, n_idx_heads, lam_init):
    bsz, t, wa = qa.shape
    wb = qb.shape[2]
    n_pages = page_table.shape[1]
    page = pool_akt.shape[2]
    past = n_pages * page
    grp = math.gcd(PAGES_ATTN, n_pages)
    n_rows = 2 * t * (wa // LANES + wb // LANES)

    def per_batch(shape):
        return pl.BlockSpec((1,) + tuple(shape[1:]), lambda b, p, pt: (b, 0, 0))

    def paged(shape, g):
        nd = len(shape) - 1
        return pl.BlockSpec((1,) + tuple(shape[1:]), lambda b, p, pt: (pt[b, p * grp + g],) + (0,) * nd)

    pools = [pool_akt, pool_av, pool_bkt, pool_bvt]
    return pl.pallas_call(
        functools.partial(_sample_attn_kernel, grp=grp, topk=topk, n_idx_heads=n_idx_heads,
                          n_col_bits=max(1, (past + page - 1).bit_length()), lam_init=lam_init),
        out_shape=[jax.ShapeDtypeStruct((bsz, t, wa), F32), jax.ShapeDtypeStruct((bsz, t, wb), F32)],
        grid_spec=pltpu.PrefetchScalarGridSpec(
            num_scalar_prefetch=1,
            grid=(bsz, n_pages // grp),
            in_specs=[pl.BlockSpec(lv.shape, lambda b, p, pt: (0, 0)),
                      pl.BlockSpec(subln_g.shape, lambda b, p, pt: (0, 0)),
                      per_batch(qa.shape), per_batch(ka.shape), per_batch(va.shape),
                      per_batch(qb.shape), per_batch(kb.shape), per_batch(vb.shape),
                      per_batch(iq.shape), per_batch(iw.shape), per_batch(ikp.shape),
                      per_batch(scores.shape),
                      pl.BlockSpec((1, t, grp * page), lambda b, p, pt: (b, 0, p))]
                     + [paged(pool.shape, g) for pool in pools for g in range(grp)],
            out_specs=[per_batch((bsz, t, wa)), per_batch((bsz, t, wb))],
            scratch_shapes=[pltpu.VMEM((t, 1), F32), pltpu.VMEM((t, 1), I32), pltpu.VMEM((t, page), F32),
                            pltpu.VMEM((n_rows // (2 * t), 2 * t, LANES), BF16),
                            pltpu.VMEM((n_rows, 1), F32), pltpu.VMEM((n_rows, 1), F32),
                            pltpu.VMEM((n_rows, LANES), F32)]),
        compiler_params=_params("parallel", "arbitrary"),
        name="sample_attn",
    )(page_table, lv, subln_g, qa, ka, va, qb, kb, vb, iq, iw, ikp, scores, scores,
      *[pool for pool in pools for _ in range(grp)])


def _outproj_kernel(oa_ref, ob_ref, gates_ref, h_ref, wa_ref, wb_ref, wo_ref, g_ref, b_ref, o_ref, *, alpha):
    ya = _dot(oa_ref[...].astype(BF16), wa_ref[...])
    yb = _dot(ob_ref[...].astype(BF16), wb_ref[...])
    d = ya.shape[1]
    mixed = gates_ref[:, :d] * ya + gates_ref[:, d:] * yb
    o_ref[...] = _layer_norm(alpha * h_ref[...] + _dot(mixed.astype(BF16), wo_ref[...]), g_ref[...], b_ref[...])


def _out_proj(oa, ob, gates, h, wa, wb, wo, g, b, alpha):
    m, d = h.shape
    tm = min(TM_PROJ, m)

    def rows(width):
        return pl.BlockSpec((tm, width), lambda i: (i, 0))

    return pl.pallas_call(
        functools.partial(_outproj_kernel, alpha=alpha),
        out_shape=jax.ShapeDtypeStruct((m, d), F32),
        grid=(m // tm,),
        in_specs=[rows(oa.shape[1]), rows(ob.shape[1]), rows(gates.shape[1]), rows(d),
                  _resident(wa.shape), _resident(wb.shape), _resident(wo.shape),
                  _resident((1, d)), _resident((1, d))],
        out_specs=rows(d),
        compiler_params=_params("parallel"),
        name="out_proj",
    )(oa, ob, gates, h, wa, wb, wo, g, b)


def _rope_tables(pos, dim):
    half = dim // 2
    freqs = ROPE_THETA ** (-jnp.arange(half, dtype=F32) / half)
    ang = pos.astype(F32)[:, None] * freqs[None, :]
    cos, sin = jnp.cos(ang), jnp.sin(ang)
    reps = LANES // dim
    return (jnp.tile(jnp.concatenate([cos, cos], axis=1), (1, reps)),
            jnp.tile(jnp.concatenate([-sin, sin], axis=1), (1, reps)))


def _split_w_in(w_in, sizes):
    offs = [0]
    for sz in sizes:
        offs.append(offs[-1] + sz)
    return [w_in[:, offs[i]:offs[i + 1]] for i in range(len(sizes))]


def _pad_heads(w, n_heads, dim):
    d = w.shape[0]
    return jnp.pad(w.reshape(d, n_heads, dim), ((0, 0), (0, 0), (0, LANES - dim))).reshape(d, n_heads * LANES)


def _pad_cols(w, width):
    return jnp.pad(w, ((0, 0), (0, width - w.shape[1])))


def kernel(x_prompt, x_sample, cache_a_k, cache_a_v, cache_b_k, cache_b_v, cache_idx_k, page_table, ln_g, ln_b,
           ffn1_w_gate_up, ffn1_w_down, w_in, b_gate, diff_lambda_vecs, diff_subln_g, w_branch_a, w_branch_b,
           w_o, ffn2_w_gate_up, ffn2_w_down):
    depth = ln_g.shape[0]
    bsz, seq, d_model = x_prompt.shape
    dec_b, dec_t, _ = x_sample.shape
    n_pool, page, ha, _, da = cache_a_k.shape[1:]
    va_dim = cache_a_v.shape[4]
    hb, db = cache_b_k.shape[3:]
    idx_dim = cache_idx_k.shape[3]
    n_pages = page_table.shape[1]
    past = n_pages * page
    d_ff = ffn1_w_down.shape[1]
    w_qa, w_va, w_b = ha * 2 * da, ha * va_dim, hb * db
    n_idx_heads = (w_in.shape[2] - 2 * w_qa - w_va - 3 * w_b - idx_dim - 2 * d_model) // (idx_dim + 1)
    sizes = (w_qa, w_qa, w_va, w_b, w_b, w_b, n_idx_heads * idx_dim, idx_dim, n_idx_heads, 2 * d_model)
    assert sum(sizes) == w_in.shape[2]
    assert 2 * da == LANES and va_dim == LANES and db == 64 and idx_dim == 64 and hb % 2 == 0
    assert w_qa == w_va and da ** -0.5 == db ** -0.5 == idx_dim ** -0.5
    alpha = (2 * depth) ** 0.25
    q_scale = (da ** -0.5 * LOG2E, idx_dim ** -0.5)
    iw_scale = n_idx_heads ** -0.5
    iw_rows = -(-n_idx_heads // 16) * 16
    widths = (w_qa, w_va, w_b, w_b, n_idx_heads * LANES, d_model)
    dims_t = (w_qa, w_b, n_idx_heads * idx_dim, idx_dim, iw_rows, d_model)
    topk_prompt = min(TOPK_MAX, seq // 4)
    topk_sample = min(TOPK_MAX, (past + dec_t) // 4)

    cos_p, sin_p = _rope_tables(jnp.arange(seq), da)
    cos_s, sin_s = _rope_tables(jnp.tile(past + jnp.arange(dec_t), dec_b), da)
    pool_akt = jnp.transpose(cache_a_k, (0, 1, 3, 4, 5, 2)).reshape(depth, n_pool, w_qa, page)
    pool_bkt = jnp.transpose(cache_b_k, (0, 1, 3, 4, 2)).reshape(depth, n_pool, w_b, page)
    pool_bvt = jnp.transpose(cache_b_v, (0, 1, 3, 4, 2)).reshape(depth, n_pool, w_b, page)
    pool_ikt = jnp.transpose(cache_idx_k, (0, 1, 3, 2))

    h_p = x_prompt.reshape(bsz * seq, d_model)
    h_s = x_sample.reshape(dec_b * dec_t, d_model)
    states_p, states_s = [], []
    for layer in range(depth):
        lam_init = _lambda_init(layer)
        row = lambda v: v.reshape(1, -1)
        ffn1 = (ffn1_w_gate_up[layer][:, :d_ff].astype(BF16), ffn1_w_gate_up[layer][:, d_ff:].astype(BF16),
                ffn1_w_down[layer].astype(BF16), row(ln_g[layer, 0]), row(ln_b[layer, 0]))
        ffn2 = (ffn2_w_gate_up[layer][:, :d_ff].astype(BF16), ffn2_w_gate_up[layer][:, d_ff:].astype(BF16),
                ffn2_w_down[layer].astype(BF16), row(ln_g[layer, 2]), row(ln_b[layer, 2]))
        wqa, wka, wva, wqb, wkb, wvb, wiq, wik, wiw, wgl = _split_w_in(w_in[layer], sizes)
        w_all = jnp.concatenate([wqa, wka, wva, wqb, wkb, wvb, _pad_heads(wiq, n_idx_heads, idx_dim),
                                 _pad_cols(wik, LANES), _pad_cols(wiw, LANES), wgl], axis=1).astype(BF16)
        w_rm = jnp.concatenate([wka, wva, wkb, _pad_cols(wik, LANES), wgl], axis=1).astype(BF16)
        w_tr = jnp.concatenate([wqa, wka, wva, wqb, wkb, wvb, wiq, wik, _pad_cols(wiw, iw_rows)],
                               axis=1).T.astype(BF16)
        bg = row(b_gate[layer])
        lv = diff_lambda_vecs[layer]
        sg = row(diff_subln_g[layer])
        outw = (w_branch_a[layer].astype(BF16), w_branch_b[layer].astype(BF16), w_o[layer].astype(BF16),
                row(ln_g[layer, 1]), row(ln_b[layer, 1]))

        h1 = _ffn_ln(h_p, *ffn1, alpha)
        (ka, va4, kb, ik, gates, qat, kat, vat, qbt, kbt, vbt, vbtb, iqt, ikt, iwt) = _in_proj_t(
            h1.reshape(bsz, seq, d_model), cos_p, sin_p, cos_p.T, sin_p.T, w_rm, w_tr, bg, dims_t, ha,
            q_scale, iw_scale)
        oa = _diff_attn_prompt(qat, ka, vat, lv, sg, lam_init)
        ob = _dsa_prompt(qbt, kb, vbtb, iqt, ik, iwt, topk_prompt, n_idx_heads)
        h2 = _out_proj(oa.reshape(-1, w_va), ob.reshape(-1, w_b), gates.reshape(-1, 2 * d_model), h1, *outw, alpha)
        h_p = _ffn_ln(h2, *ffn2, alpha)
        states_p.append((jnp.transpose(kat.reshape(bsz, ha, 2, da, seq), (0, 4, 1, 2, 3)), va4,
                         jnp.transpose(kbt.reshape(bsz, hb, db, seq), (0, 3, 1, 2)),
                         jnp.transpose(vbt.reshape(bsz, hb, db, seq), (0, 3, 1, 2)),
                         jnp.transpose(ikt, (0, 2, 1))))

        h1 = _ffn_ln(h_s, *ffn1, alpha)
        outs = _in_proj(h1.reshape(1, dec_b * dec_t, d_model), cos_s, sin_s, w_all, bg, widths, idx_dim,
                        q_scale, iw_scale)
        qa, ka, va, qb, kb, vb, iq, ikp, ik, iw, gates = [o.reshape(dec_b, dec_t, -1) for o in outs]
        scores = _sample_scores(page_table, iq, iw, pool_ikt[layer], n_idx_heads)
        oa, ob = _sample_attn(page_table, lv, sg, qa, ka, va, qb, kb, vb, iq, iw, ikp, scores,
                              pool_akt[layer], cache_a_v[layer], pool_bkt[layer], pool_bvt[layer],
                              topk_sample, n_idx_heads, lam_init)
        h2 = _out_proj(oa.reshape(-1, w_va), ob.reshape(-1, w_b), gates.reshape(-1, 2 * d_model), h1, *outw, alpha)
        h_s = _ffn_ln(h2, *ffn2, alpha)
        states_s.append((ka.reshape(dec_b, dec_t, ha, 2, da), va.reshape(dec_b, dec_t, ha, va_dim),
                         kb.reshape(dec_b, dec_t, hb, db), vb.reshape(dec_b, dec_t, hb, db), ik))

    outs = [jnp.stack(s) for s in zip(*states_p)] + [jnp.stack(s) for s in zip(*states_s)]
    return (h_p.reshape(bsz, seq, d_model), h_s.reshape(dec_b, dec_t, d_model), *outs)
```
